```python
import jax, jax.numpy as jnp
from jax import lax
import numpy as np

D_MODEL = 1024
BATCH = 4
SEQ = 8192
DEPTH = 1

GRID_W = 64
NA_HEADS = 8
NA_HEAD_DIM = 64
D_ATTN = NA_HEADS * NA_HEAD_DIM
NA_KH = 8
NA_KW = 16
Q_BLOCK_W = 16
K_BLOCK_W = 2 * NA_KW
N_COL_BLOCKS = GRID_W // Q_BLOCK_W
CONV_CH = 512
CONV_K = 31
D_FF = 2816
FFN_CONV_K = 3
D_IN = 3 * D_ATTN + 2 * CONV_CH + 2 * D_MODEL
EPS = 1e-6
NEG = -1e30

kernel_name = "hybrid_natten_conformer_convffn_block"


def _rmsnorm(x, g):
    xf = x.astype(jnp.float32)
    y = xf * lax.rsqrt(jnp.mean(xf * xf, axis=-1, keepdims=True) + EPS)
    return (y * g.astype(jnp.float32)).astype(x.dtype)


def _layernorm(x, g, b):
    xf = x.astype(jnp.float32)
    mu = jnp.mean(xf, axis=-1, keepdims=True)
    var = jnp.mean(jnp.square(xf - mu), axis=-1, keepdims=True)
    y = (xf - mu) * lax.rsqrt(var + EPS)
    return (y * g.astype(jnp.float32) + b.astype(jnp.float32)).astype(x.dtype)


def _dwconv_centred(x, w, b):
    k = w.shape[0]
    pad = k // 2
    y = lax.conv_general_dilated(
        x, w[:, None, :].astype(x.dtype), window_strides=(1,), padding=[(pad, pad)],
        dimension_numbers=("NWC", "WIO", "NWC"), feature_group_count=x.shape[-1])
    return y + b.astype(x.dtype)


def _neighbourhood_attention(q, k, v, rpb):
    bsz, s, h, dh = q.shape
    rows = s // GRID_W
    kh = min(NA_KH, rows)
    scale = NA_HEAD_DIM ** -0.5
    qg = q.reshape(bsz, rows, GRID_W, h, dh)
    kg = k.reshape(bsz, rows, GRID_W, h, dh)
    vg = v.reshape(bsz, rows, GRID_W, h, dh)

    col = jnp.arange(GRID_W)
    col_start = jnp.clip(col - NA_KW // 2, 0, GRID_W - NA_KW)
    blk_start = jnp.minimum(col_start[jnp.arange(N_COL_BLOCKS) * Q_BLOCK_W],
                            GRID_W - K_BLOCK_W)
    key_col = blk_start[:, None] + jnp.arange(K_BLOCK_W)
    q_col = col.reshape(N_COL_BLOCKS, Q_BLOCK_W)
    cs = col_start.reshape(N_COL_BLOCKS, Q_BLOCK_W)
    kc = key_col[:, None, :]
    in_win = (kc >= cs[..., None]) & (kc < cs[..., None] + NA_KW)
    col_off = jnp.clip(kc - q_col[..., None] + NA_KW - 1, 0, 2 * NA_KW - 2)
    rpb_col = rpb.astype(jnp.float32)[:, :, col_off]
    mask = in_win[None, None, :, :, None, :]

    def one_row(r):
        rs = jnp.clip(r - kh // 2, 0, rows - kh)
        q_r = lax.dynamic_index_in_dim(qg, r, axis=1, keepdims=False)
        q_r = q_r.reshape(bsz, N_COL_BLOCKS, Q_BLOCK_W, h, dh)
        k_r = lax.dynamic_slice_in_dim(kg, rs, kh, axis=1)[:, :, key_col]
        v_r = lax.dynamic_slice_in_dim(vg, rs, kh, axis=1)[:, :, key_col]
        row_off = rs + jnp.arange(kh) - r + NA_KH - 1
        bias = jnp.take(rpb_col, row_off, axis=1).transpose(0, 2, 3, 1, 4)
        sc = jnp.einsum("bnqhd,bknjhd->bhnqkj", q_r, k_r).astype(jnp.float32) * scale
        sc = jnp.where(mask, sc + bias[None], NEG)
        p = jax.nn.softmax(sc.reshape(bsz, h, N_COL_BLOCKS, Q_BLOCK_W, kh * K_BLOCK_W), axis=-1)
        p = p.reshape(sc.shape).astype(v.dtype)
        o = jnp.einsum("bhnqkj,bknjhd->bnqhd", p, v_r)
        return o.reshape(bsz, GRID_W, h * dh)

    out = lax.map(one_row, jnp.arange(rows))
    return out.transpose(1, 0, 2, 3).reshape(bsz, s, h * dh)


def setup_inputs(seed: int = 0) -> dict:
    key = jax.random.key(seed)
    ks = jax.random.split(key, 20)
    n = jax.random.normal
    f32 = jnp.float32
    L = DEPTH
    return {
        "x": n(ks[0], (BATCH, SEQ, D_MODEL), f32),
        "norm1_g": 1.0 + 0.05 * n(ks[1], (L, D_MODEL), f32),
        "w_in": n(ks[2], (L, D_MODEL, D_IN), f32) * D_MODEL ** -0.5,
        "b_in": 0.01 * n(ks[3], (L, D_IN), f32),
        "rpb": 0.1 * n(ks[4], (L, NA_HEADS, 2 * NA_KH - 1, 2 * NA_KW - 1), f32),
        "w_na_out": n(ks[5], (L, D_ATTN, D_MODEL), f32) * D_ATTN ** -0.5,
        "conv_dw_w": n(ks[6], (L, CONV_K, CONV_CH), f32) * CONV_K ** -0.5,
        "conv_dw_b": 0.01 * n(ks[7], (L, CONV_CH), f32),
        "conv_ln_g": 1.0 + 0.05 * n(ks[8], (L, CONV_CH), f32),
        "conv_ln_b": 0.01 * n(ks[9], (L, CONV_CH), f32),
        "w_conv_out": n(ks[10], (L, CONV_CH, D_MODEL), f32) * CONV_CH ** -0.5,
        "w_out": n(ks[11], (L, D_MODEL, D_MODEL), f32) * D_MODEL ** -0.5,
        "norm2_g": 1.0 + 0.05 * n(ks[12], (L, D_MODEL), f32),
        "w_up": n(ks[13], (L, D_MODEL, 2 * D_FF), f32) * D_MODEL ** -0.5,
        "ffn_dw_w": n(ks[14], (L, FFN_CONV_K, 2 * D_FF), f32) * FFN_CONV_K ** -0.5,
        "ffn_dw_b": 0.01 * n(ks[15], (L, 2 * D_FF), f32),
        "w_down": n(ks[16], (L, D_FF, D_MODEL), f32) * D_FF ** -0.5,
        "norm_f_g": 1.0 + 0.05 * n(ks[17], (D_MODEL,), f32),
    }


def reference(x, norm1_g, w_in, b_in, rpb, w_na_out, conv_dw_w, conv_dw_b, conv_ln_g,
              conv_ln_b, w_conv_out, w_out, norm2_g, w_up, ffn_dw_w, ffn_dw_b, w_down,
              norm_f_g):
    bsz, s, _ = x.shape
    splits = np.cumsum([D_ATTN, D_ATTN, D_ATTN, CONV_CH, CONV_CH, D_MODEL]).tolist()
    for l in range(DEPTH):
        h = _rmsnorm(x, norm1_g[l])
        z = h @ w_in[l] + b_in[l]
        q, k, v, c_a, c_b, g_a, g_b = jnp.split(z, splits, axis=-1)
        qh = q.reshape(bsz, s, NA_HEADS, NA_HEAD_DIM)
        kh = k.reshape(bsz, s, NA_HEADS, NA_HEAD_DIM)
        vh = v.reshape(bsz, s, NA_HEADS, NA_HEAD_DIM)
        br_a = _neighbourhood_attention(qh, kh, vh, rpb[l]) @ w_na_out[l]
        u = c_a * jax.nn.sigmoid(c_b)
        u = _dwconv_centred(u, conv_dw_w[l], conv_dw_b[l])
        u = jax.nn.silu(_layernorm(u, conv_ln_g[l], conv_ln_b[l]))
        br_b = u @ w_conv_out[l]
        merged = jax.nn.sigmoid(g_a) * br_a + jax.nn.sigmoid(g_b) * br_b
        x = x + merged @ w_out[l]
        h2 = _rmsnorm(x, norm2_g[l])
        up = _dwconv_centred(h2 @ w_up[l], ffn_dw_w[l], ffn_dw_b[l])
        gate, val = jnp.split(up, 2, axis=-1)
        x = x + (jax.nn.gelu(gate, approximate=False) * val) @ w_down[l]
    return _rmsnorm(x, norm_f_g)
```

```python
import functools

import jax
import jax.numpy as jnp
import numpy as np
from jax import lax
from jax.experimental import pallas as pl
from jax.experimental.pallas import tpu as pltpu

F32 = jnp.float32
BF16 = jnp.bfloat16

D_MODEL = 1024
GRID_W = 64
NA_HEADS = 8
NA_HEAD_DIM = 64
D_ATTN = NA_HEADS * NA_HEAD_DIM
NA_KH = 8
NA_KW = 16
CONV_CH = 512
CONV_K = 31
D_FF = 2816
FFN_CONV_K = 3
EPS = 1e-6
NEG = -1e30

LANES = 128
SUBLANES = 8
BF16_SUBLANES = 16
VMEM_LIMIT_BYTES = 56 * 1024 * 1024

TM_INPROJ = 512
TM_MIXER = 256
TM_FFN = 512
NATTEN_ROWS_PER_STEP = 8
CONV_ROW_CHUNK = 32
CONV_HALO = 16
FFN_HALO = 8
FFN_CHUNK = 256


def _rms_scale(x):
    return x * lax.rsqrt(jnp.mean(x * x, axis=-1, keepdims=True) + EPS)


def _gelu_exact(x):
    return 0.5 * x * (1.0 + lax.erf(x * np.float32(np.sqrt(0.5))))


def _inproj_kernel(x_ref, g_ref, w_ref, b_ref, q_ref, k_ref, v_ref, u_ref):
    h = (_rms_scale(x_ref[...]) * g_ref[...]).astype(BF16)

    def proj(n):
        cols = slice(n * D_ATTN, (n + 1) * D_ATTN)
        return jnp.dot(h, w_ref[:, cols], preferred_element_type=F32) + b_ref[:, cols]

    q_ref[...] = (proj(0) * (NA_HEAD_DIM ** -0.5)).astype(BF16)
    k_ref[...] = proj(1).astype(BF16)
    v_ref[...] = proj(2).astype(BF16)
    u_ref[...] = (proj(3) * jax.nn.sigmoid(proj(4))).astype(BF16)


def _inproj(x2d, g, w, b):
    m = x2d.shape[0]
    tm = TM_INPROJ
    n_in = w.shape[1]
    const = lambda i: (0, 0)
    out = jax.ShapeDtypeStruct((m, D_ATTN), BF16)
    row_spec = pl.BlockSpec((tm, D_ATTN), lambda i: (i, 0))
    return pl.pallas_call(
        _inproj_kernel,
        grid=(m // tm,),
        in_specs=[
            pl.BlockSpec((tm, D_MODEL), lambda i: (i, 0)),
            pl.BlockSpec((1, D_MODEL), const),
            pl.BlockSpec((D_MODEL, n_in), const),
            pl.BlockSpec((1, n_in), const),
        ],
        out_specs=[row_spec, row_spec, row_spec, row_spec],
        out_shape=[out, out, out, out],
        compiler_params=pltpu.CompilerParams(
            dimension_semantics=("arbitrary",), vmem_limit_bytes=VMEM_LIMIT_BYTES),
        name="inproj",
    )(x2d, g, w, b)


def _natten_bias_table(rpb):
    col = jnp.arange(GRID_W)
    col_start = jnp.clip(col - NA_KW // 2, 0, GRID_W - NA_KW)
    kc = col[None, :]
    in_win = (kc >= col_start[:, None]) & (kc < col_start[:, None] + NA_KW)
    col_off = jnp.clip(kc - col[:, None] + NA_KW - 1, 0, 2 * NA_KW - 2)
    t = rpb.astype(F32)[:, :, col_off]
    t = jnp.where(in_win[None, None], t, NEG)
    nxt = jnp.concatenate([t[:, 1:], jnp.full_like(t[:, :1], NEG)], axis=1)
    return jnp.concatenate([t, nxt], axis=-1)


def _natten_kernel(q_ref, k_ref, v_ref, tb_ref, o_ref, *, rows):
    step = pl.program_id(1)
    lane = lax.broadcasted_iota(jnp.int32, (GRID_W, LANES), 1)
    lo_half = lane < NA_HEAD_DIM
    n_keys = NA_KH * GRID_W

    def one_row(rr, carry):
        r = step * NATTEN_ROWS_PER_STEP + rr
        rs = jnp.clip(r - NA_KH // 2, 0, rows - NA_KH)
        base = rs - r + NA_KH - 1
        q0 = pl.multiple_of(rr * GRID_W, GRID_W)
        k0 = pl.multiple_of(rs * GRID_W, GRID_W)
        outs = []
        for p in range(NA_HEADS // 2):
            cols = slice(p * LANES, (p + 1) * LANES)
            qp = q_ref[0, pl.ds(q0, GRID_W), cols]
            kp = k_ref[0, pl.ds(k0, n_keys), cols]
            vp = v_ref[0, pl.ds(k0, n_keys), cols]
            zero = jnp.zeros_like(qp)
            q2 = jnp.concatenate([jnp.where(lo_half, qp, zero), jnp.where(lo_half, zero, qp)], axis=0)
            s = lax.dot_general(q2, kp, (((1,), (1,)), ((), ())), preferred_element_type=F32)
            bias = jnp.concatenate(
                [jnp.concatenate([tb_ref[2 * p + hh, base + 2 * j] for j in range(NA_KH // 2)], axis=1)
                 for hh in range(2)], axis=0)
            s = s + bias
            m = jnp.max(s, axis=-1, keepdims=True)
            e = jnp.exp(s - m)
            l = jnp.sum(e, axis=-1, keepdims=True)
            o2 = jnp.dot(e.astype(BF16), vp, preferred_element_type=F32) / l
            outs.append(jnp.where(lo_half, o2[:GRID_W], o2[GRID_W:]))
        o_ref[0, pl.ds(q0, GRID_W), :] = jnp.concatenate(outs, axis=1).astype(o_ref.dtype)
        return carry

    lax.fori_loop(0, NATTEN_ROWS_PER_STEP, one_row, 0)


def _natten(q, k, v, table):
    bsz, s, _ = q.shape
    rows = s // GRID_W
    assert rows >= NA_KH and rows % NATTEN_ROWS_PER_STEP == 0
    tq = NATTEN_ROWS_PER_STEP * GRID_W
    return pl.pallas_call(
        functools.partial(_natten_kernel, rows=rows),
        grid=(bsz, rows // NATTEN_ROWS_PER_STEP),
        in_specs=[
            pl.BlockSpec((1, tq, D_ATTN), lambda b, i: (b, i, 0)),
            pl.BlockSpec((1, s, D_ATTN), lambda b, i: (b, 0, 0)),
            pl.BlockSpec((1, s, D_ATTN), lambda b, i: (b, 0, 0)),
            pl.BlockSpec(table.shape, lambda b, i: (0, 0, 0, 0)),
        ],
        out_specs=pl.BlockSpec((1, tq, D_ATTN), lambda b, i: (b, i, 0)),
        out_shape=jax.ShapeDtypeStruct((bsz, s, D_ATTN), BF16),
        compiler_params=pltpu.CompilerParams(
            dimension_semantics=("arbitrary", "arbitrary"), vmem_limit_bytes=VMEM_LIMIT_BYTES),
        name="natten",
    )(q, k, v, table)


def _mixer_kernel(x_ref, a_ref, u_ref, up_ref, un_ref, g1_ref, wg_ref, bg_ref, cw_ref, cb_ref,
                  lg_ref, lb_ref, wna_ref, wco_ref, wo_ref, o_ref, ext_ref, u2_ref, *, tiles_per_seq):
    tm = x_ref.shape[0]
    t = pl.program_id(0) % tiles_per_seq
    not_first = (t != 0).astype(F32)
    not_last = (t != tiles_per_seq - 1).astype(F32)

    ext_ref[0:CONV_HALO, :] = up_ref[...].astype(F32) * not_first
    ext_ref[CONV_HALO:CONV_HALO + tm, :] = u_ref[...].astype(F32)
    ext_ref[CONV_HALO + tm:, :] = un_ref[...].astype(F32) * not_last

    rc = CONV_ROW_CHUNK
    pad = CONV_K // 2
    first = CONV_HALO - pad
    span = rc + 2 * CONV_HALO

    def conv_chunk(c, carry):
        r0 = pl.multiple_of(c * rc, rc)
        parts = []
        for lb in range(CONV_CH // LANES):
            cols = slice(lb * LANES, (lb + 1) * LANES)
            win = ext_ref[pl.ds(r0, span), cols]
            acc = jnp.zeros((rc, LANES), F32) + cb_ref[:, cols]
            for sh in range(SUBLANES):
                taps = [kk for kk in range(CONV_K) if (first + kk) % SUBLANES == sh]
                if not taps:
                    continue
                hi = max(first + kk for kk in taps) + rc
                shifted = win[sh:hi]
                for kk in taps:
                    lo = first + kk - sh
                    acc = acc + cw_ref[kk:kk + 1, cols] * shifted[lo:lo + rc]
            parts.append(acc)
        y = jnp.concatenate(parts, axis=1)
        mu = jnp.mean(y, axis=-1, keepdims=True)
        d = y - mu
        var = jnp.mean(d * d, axis=-1, keepdims=True)
        z = d * lax.rsqrt(var + EPS) * lg_ref[...] + lb_ref[...]
        u2_ref[pl.ds(r0, rc), :] = (z * jax.nn.sigmoid(z)).astype(BF16)
        return carry

    lax.fori_loop(0, tm // rc, conv_chunk, 0)

    x = x_ref[...]
    h = (_rms_scale(x) * g1_ref[...]).astype(BF16)
    gates = jax.nn.sigmoid(jnp.dot(h, wg_ref[...], preferred_element_type=F32) + bg_ref[...])
    br_a = jnp.dot(a_ref[...], wna_ref[...], preferred_element_type=F32)
    br_b = jnp.dot(u2_ref[...], wco_ref[...], preferred_element_type=F32)
    merged = gates[:, :D_MODEL] * br_a + gates[:, D_MODEL:] * br_b
    o_ref[...] = x + jnp.dot(merged.astype(BF16), wo_ref[...], preferred_element_type=F32)


def _mixer(x2d, a2d, u2d, seq, g1, wg, bg, cw, cb, lg, lb, wna, wco, wo):
    m = x2d.shape[0]
    tm = TM_MIXER
    assert seq % tm == 0 and tm % CONV_ROW_CHUNK == 0
    tiles_per_seq = seq // tm
    hb = tm // CONV_HALO
    n_hb = m // CONV_HALO
    const = lambda i: (0, 0)
    full = lambda arr: pl.BlockSpec(arr.shape, const)
    return pl.pallas_call(
        functools.partial(_mixer_kernel, tiles_per_seq=tiles_per_seq),
        grid=(m // tm,),
        in_specs=[
            pl.BlockSpec((tm, D_MODEL), lambda i: (i, 0)),
            pl.BlockSpec((tm, D_ATTN), lambda i: (i, 0)),
            pl.BlockSpec((tm, CONV_CH), lambda i: (i, 0)),
            pl.BlockSpec((CONV_HALO, CONV_CH), lambda i: (jnp.maximum(i * hb - 1, 0), 0)),
            pl.BlockSpec((CONV_HALO, CONV_CH), lambda i: (jnp.minimum((i + 1) * hb, n_hb - 1), 0)),
            full(g1), full(wg), full(bg), full(cw), full(cb), full(lg), full(lb),
            full(wna), full(wco), full(wo),
        ],
        out_specs=pl.BlockSpec((tm, D_MODEL), lambda i: (i, 0)),
        out_shape=jax.ShapeDtypeStruct((m, D_MODEL), F32),
        scratch_shapes=[
            pltpu.VMEM((tm + 2 * CONV_HALO, CONV_CH), F32),
            pltpu.VMEM((tm, CONV_CH), BF16),
        ],
        compiler_params=pltpu.CompilerParams(
            dimension_semantics=("arbitrary",), vmem_limit_bytes=VMEM_LIMIT_BYTES),
        name="mixer",
    )(x2d, a2d, u2d, u2d, u2d, g1, wg, bg, cw, cb, lg, lb, wna, wco, wo)


def _ffn_kernel(x_ref, xp_ref, xn_ref, g2_ref, wup_ref, cw_ref, cb_ref, wdn_ref, gf_ref, o_ref,
                act_ref, *, tiles_per_seq, final_norm):
    tm = x_ref.shape[0]
    t = pl.program_id(0) % tiles_per_seq
    not_first = (t != 0).astype(F32)
    not_last = (t != tiles_per_seq - 1).astype(F32)

    x = x_ref[...]
    g2 = g2_ref[...]
    h_ext = jnp.concatenate([
        (_rms_scale(xp_ref[...]) * g2 * not_first).astype(BF16),
        (_rms_scale(x) * g2).astype(BF16),
        (_rms_scale(xn_ref[...]) * g2 * not_last).astype(BF16),
    ], axis=0)

    def conv_up(col0):
        cols = slice(col0, col0 + FFN_CHUNK)
        up = jnp.dot(h_ext, wup_ref[:, cols], preferred_element_type=F32)
        y = cb_ref[:, cols]
        for kk in range(FFN_CONV_K):
            lo = FFN_HALO - FFN_CONV_K // 2 + kk
            y = y + cw_ref[kk:kk + 1, cols] * up[lo:lo + tm]
        return y

    for c in range(D_FF // FFN_CHUNK):
        gate = conv_up(c * FFN_CHUNK)
        val = conv_up(D_FF + c * FFN_CHUNK)
        act_ref[:, c * FFN_CHUNK:(c + 1) * FFN_CHUNK] = (_gelu_exact(gate) * val).astype(BF16)

    y = x + jnp.dot(act_ref[...], wdn_ref[...], preferred_element_type=F32)
    if final_norm:
        y = _rms_scale(y) * gf_ref[...]
    o_ref[...] = y


def _ffn(x2d, seq, g2, wup, cw, cb, wdn, gf, final_norm):
    m = x2d.shape[0]
    tm = TM_FFN
    assert seq % tm == 0
    tiles_per_seq = seq // tm
    hb = tm // FFN_HALO
    n_hb = m // FFN_HALO
    const = lambda i: (0, 0)
    full = lambda arr: pl.BlockSpec(arr.shape, const)
    resident = lambda arr: pl.BlockSpec(arr.shape, const, pipeline_mode=pl.Buffered(1))
    return pl.pallas_call(
        functools.partial(_ffn_kernel, tiles_per_seq=tiles_per_seq, final_norm=final_norm),
        grid=(m // tm,),
        in_specs=[
            pl.BlockSpec((tm, D_MODEL), lambda i: (i, 0)),
            pl.BlockSpec((FFN_HALO, D_MODEL), lambda i: (jnp.maximum(i * hb - 1, 0), 0)),
            pl.BlockSpec((FFN_HALO, D_MODEL), lambda i: (jnp.minimum((i + 1) * hb, n_hb - 1), 0)),
            full(g2), resident(wup), full(cw), full(cb), resident(wdn), full(gf),
        ],
        out_specs=pl.BlockSpec((tm, D_MODEL), lambda i: (i, 0)),
        out_shape=jax.ShapeDtypeStruct((m, D_MODEL), F32),
        scratch_shapes=[pltpu.VMEM((tm, D_FF), BF16)],
        compiler_params=pltpu.CompilerParams(
            dimension_semantics=("arbitrary",), vmem_limit_bytes=VMEM_LIMIT_BYTES),
        name="ffn",
    )(x2d, x2d, x2d, g2, wup, cw, cb, wdn, gf)


def kernel(x, norm1_g, w_in, b_in, rpb, w_na_out, conv_dw_w, conv_dw_b, conv_ln_g, conv_ln_b,
           w_conv_out, w_out, norm2_g, w_up, ffn_dw_w, ffn_dw_b, w_down, norm_f_g):
    bsz, seq, d = x.shape
    depth = w_in.shape[0]
    n_tok = 3 * D_ATTN + 2 * CONV_CH
    row = lambda v: v.reshape(1, -1).astype(F32)
    x2d = x.reshape(bsz * seq, d)
    for l in range(depth):
        w_in_l = w_in[l].astype(BF16)
        b_in_l = row(b_in[l])
        g1 = row(norm1_g[l])
        q, k, v, u = _inproj(x2d, g1, w_in_l[:, :n_tok], b_in_l[:, :n_tok])
        shp = (bsz, seq, D_ATTN)
        a = _natten(q.reshape(shp), k.reshape(shp), v.reshape(shp), _natten_bias_table(rpb[l]))
        x2d = _mixer(
            x2d, a.reshape(bsz * seq, D_ATTN), u, seq, g1, w_in_l[:, n_tok:], b_in_l[:, n_tok:],
            conv_dw_w[l].astype(F32), row(conv_dw_b[l]), row(conv_ln_g[l]), row(conv_ln_b[l]),
            w_na_out[l].astype(BF16), w_conv_out[l].astype(BF16), w_out[l].astype(BF16))
        x2d = _ffn(
            x2d, seq, row(norm2_g[l]), w_up[l].astype(BF16), ffn_dw_w[l].astype(F32),
            row(ffn_dw_b[l]), w_down[l].astype(BF16), row(norm_f_g), final_norm=(l == depth - 1))
    return x2d.reshape(bsz, seq, d)
```

```python
import functools

import jax
import jax.numpy as jnp
import numpy as np
from jax import lax
from jax.experimental import pallas as pl
from jax.experimental.pallas import tpu as pltpu

F32 = jnp.float32
BF16 = jnp.bfloat16

D_MODEL = 1024
GRID_W = 64
NA_HEADS = 8
NA_HEAD_DIM = 64
D_ATTN = NA_HEADS * NA_HEAD_DIM
NA_KH = 8
NA_KW = 16
CONV_CH = 512
CONV_K = 31
D_FF = 2816
FFN_CONV_K = 3
EPS = 1e-6
NEG = -1e30

LANES = 128
SUBLANES = 8
BF16_SUBLANES = 16
VMEM_LIMIT_BYTES = 56 * 1024 * 1024

TM_INPROJ = 512
TM_MIXER = 256
TM_FFN = 512
NATTEN_ROWS_PER_STEP = 8
NATTEN_ROWS_PER_ITER = 2
CONV_VREG_BLOCK = 8
GATE_CHUNK = 256
CONV_HALO = 16
FFN_HALO = 8
FFN_CHUNK = 256


def _rms_scale(x):
    return x * lax.rsqrt(jnp.mean(x * x, axis=-1, keepdims=True) + EPS)


def _gelu_exact(x):
    return 0.5 * x * (1.0 + lax.erf(x * np.float32(np.sqrt(0.5))))


def _inproj_kernel(x_ref, g_ref, w_ref, b_ref, q_ref, k_ref, v_ref, u_ref):
    h = (_rms_scale(x_ref[...]) * g_ref[...]).astype(BF16)

    def proj(n):
        cols = slice(n * D_ATTN, (n + 1) * D_ATTN)
        return jnp.dot(h, w_ref[:, cols], preferred_element_type=F32) + b_ref[:, cols]

    q_ref[...] = (proj(0) * (NA_HEAD_DIM ** -0.5)).astype(BF16)
    k_ref[...] = proj(1).astype(BF16)
    v_ref[...] = proj(2).astype(BF16)
    u_ref[...] = (proj(3) * jax.nn.sigmoid(proj(4))).astype(BF16)


def _inproj(x2d, g, w, b):
    m = x2d.shape[0]
    tm = TM_INPROJ
    n_in = w.shape[1]
    const = lambda i: (0, 0)
    out = jax.ShapeDtypeStruct((m, D_ATTN), BF16)
    row_spec = pl.BlockSpec((tm, D_ATTN), lambda i: (i, 0))
    return pl.pallas_call(
        _inproj_kernel,
        grid=(m // tm,),
        in_specs=[
            pl.BlockSpec((tm, D_MODEL), lambda i: (i, 0)),
            pl.BlockSpec((1, D_MODEL), const),
            pl.BlockSpec((D_MODEL, n_in), const),
            pl.BlockSpec((1, n_in), const),
        ],
        out_specs=[row_spec, row_spec, row_spec, row_spec],
        out_shape=[out, out, out, out],
        compiler_params=pltpu.CompilerParams(
            dimension_semantics=("arbitrary",), vmem_limit_bytes=VMEM_LIMIT_BYTES),
        name="inproj",
    )(x2d, g, w, b)


def _natten_bias_table(rpb):
    col = jnp.arange(GRID_W)
    col_start = jnp.clip(col - NA_KW // 2, 0, GRID_W - NA_KW)
    kc = col[None, :]
    in_win = (kc >= col_start[:, None]) & (kc < col_start[:, None] + NA_KW)
    col_off = jnp.clip(kc - col[:, None] + NA_KW - 1, 0, 2 * NA_KW - 2)
    t = rpb.astype(F32)[:, :, col_off]
    t = jnp.where(in_win[None, None], t, NEG)
    nxt = jnp.concatenate([t[:, 1:], jnp.full_like(t[:, :1], NEG)], axis=1)
    return jnp.concatenate([t, nxt], axis=-1)


def _natten_kernel(q_ref, k_ref, v_ref, tb_ref, o_ref, *, rows):
    step = pl.program_id(1)
    lane = lax.broadcasted_iota(jnp.int32, (GRID_W, LANES), 1)
    lo_half = lane < NA_HEAD_DIM
    n_keys = NA_KH * GRID_W

    def row_group(it, carry):
        jobs = []
        for sub in range(NATTEN_ROWS_PER_ITER):
            rr = it * NATTEN_ROWS_PER_ITER + sub
            r = step * NATTEN_ROWS_PER_STEP + rr
            rs = jnp.clip(r - NA_KH // 2, 0, rows - NA_KH)
            base = rs - r + NA_KH - 1
            q0 = pl.multiple_of(rr * GRID_W, GRID_W)
            k0 = pl.multiple_of(rs * GRID_W, GRID_W)
            jobs += [(q0, k0, base, p) for p in range(NA_HEADS // 2)]

        scores = []
        for q0, k0, base, p in jobs:
            cols = slice(p * LANES, (p + 1) * LANES)
            qp = q_ref[0, pl.ds(q0, GRID_W), cols]
            kp = k_ref[0, pl.ds(k0, n_keys), cols]
            zero = jnp.zeros_like(qp)
            q2 = jnp.concatenate([jnp.where(lo_half, qp, zero), jnp.where(lo_half, zero, qp)], axis=0)
            s = lax.dot_general(q2, kp, (((1,), (1,)), ((), ())), preferred_element_type=F32)
            bias = jnp.concatenate(
                [jnp.concatenate([tb_ref[2 * p + hh, base + 2 * j] for j in range(NA_KH // 2)], axis=1)
                 for hh in range(2)], axis=0)
            scores.append(s + bias)

        probs = []
        for s in scores:
            e = jnp.exp(s - jnp.max(s, axis=-1, keepdims=True))
            probs.append((e.astype(BF16), jnp.sum(e, axis=-1, keepdims=True)))

        outs = []
        for (e, l), (q0, k0, base, p) in zip(probs, jobs):
            vp = v_ref[0, pl.ds(k0, n_keys), p * LANES:(p + 1) * LANES]
            o2 = jnp.dot(e, vp, preferred_element_type=F32) / l
            outs.append(jnp.where(lo_half, o2[:GRID_W], o2[GRID_W:]))

        pairs = NA_HEADS // 2
        for sub in range(NATTEN_ROWS_PER_ITER):
            q0 = jobs[sub * pairs][0]
            o_ref[0, pl.ds(q0, GRID_W), :] = jnp.concatenate(
                outs[sub * pairs:(sub + 1) * pairs], axis=1).astype(o_ref.dtype)
        return carry

    lax.fori_loop(0, NATTEN_ROWS_PER_STEP // NATTEN_ROWS_PER_ITER, row_group, 0)


def _natten(q, k, v, table):
    bsz, s, _ = q.shape
    rows = s // GRID_W
    assert rows >= NA_KH and rows % NATTEN_ROWS_PER_STEP == 0
    tq = NATTEN_ROWS_PER_STEP * GRID_W
    return pl.pallas_call(
        functools.partial(_natten_kernel, rows=rows),
        grid=(bsz, rows // NATTEN_ROWS_PER_STEP),
        in_specs=[
            pl.BlockSpec((1, tq, D_ATTN), lambda b, i: (b, i, 0)),
            pl.BlockSpec((1, s, D_ATTN), lambda b, i: (b, 0, 0)),
            pl.BlockSpec((1, s, D_ATTN), lambda b, i: (b, 0, 0)),
            pl.BlockSpec(table.shape, lambda b, i: (0, 0, 0, 0)),
        ],
        out_specs=pl.BlockSpec((1, tq, D_ATTN), lambda b, i: (b, i, 0)),
        out_shape=jax.ShapeDtypeStruct((bsz, s, D_ATTN), BF16),
        compiler_params=pltpu.CompilerParams(
            dimension_semantics=("arbitrary", "arbitrary"), vmem_limit_bytes=VMEM_LIMIT_BYTES),
        name="natten",
    )(q, k, v, table)


def _mixer_kernel(x_ref, a_ref, u_ref, up_ref, un_ref, g1_ref, wg_ref, bg_ref, cw_ref, cb_ref,
                  lg_ref, lb_ref, wna_ref, wco_ref, wo_ref, pin_ref, pout_ref, o_ref, d_ref, y_ref,
                  gates_ref, h_ref, *, tiles_per_seq):
    tm = x_ref.shape[0]
    run = tm // SUBLANES
    n_lb = CONV_CH // LANES
    t = pl.program_id(0) % tiles_per_seq

    zero_halo = jnp.zeros(up_ref.shape, BF16)
    ext = jnp.concatenate([
        jnp.where(t != 0, up_ref[...], zero_halo),
        u_ref[...],
        jnp.where(t != tiles_per_seq - 1, un_ref[...], zero_halo)], axis=0)

    d = jnp.dot(pin_ref[...], ext, preferred_element_type=F32)
    for lb in range(n_lb):
        d_ref[lb] = d[:, lb * LANES:(lb + 1) * LANES]

    x = x_ref[...]
    h_ref[...] = (_rms_scale(x) * g1_ref[...]).astype(BF16)

    n_acc = CONV_VREG_BLOCK
    blocks_per_lb = run // n_acc
    n_chunks = gates_ref.shape[0]
    blocks_per_chunk = n_lb * blocks_per_lb // n_chunks

    def conv_block(blk):
        lb = blk // blocks_per_lb
        row0 = pl.multiple_of((blk % blocks_per_lb) * (n_acc * SUBLANES), n_acc * SUBLANES)
        w = [cw_ref[lb, kk:kk + 1, :] for kk in range(CONV_K)]
        accs = [cb_ref[lb] + jnp.zeros((SUBLANES, LANES), F32)] * n_acc
        for i in range(n_acc + CONV_K - 1):
            dv = d_ref[lb, pl.ds(row0 + i * SUBLANES, SUBLANES), :]
            for j in range(n_acc):
                if 0 <= i - j < CONV_K:
                    accs[j] = accs[j] + w[i - j] * dv
        return lb, row0, accs

    def chunk(c, carry):
        convs = [conv_block(c * blocks_per_chunk + b) for b in range(blocks_per_chunk)]
        g = jnp.dot(h_ref[...], wg_ref[c], preferred_element_type=F32) + bg_ref[c]
        for lb, row0, accs in convs:
            for j in range(n_acc):
                y_ref[lb, pl.ds(row0 + j * SUBLANES, SUBLANES), :] = accs[j]
        gates_ref[c] = g
        return carry

    lax.fori_loop(0, n_chunks, chunk, 0)

    half = n_chunks // 2
    gate = lambda first: jax.nn.sigmoid(
        jnp.concatenate([gates_ref[first + n] for n in range(half)], axis=1))
    gated_a = gate(0) * jnp.dot(a_ref[...], wna_ref[...], preferred_element_type=F32)

    y = jnp.concatenate([y_ref[lb] for lb in range(n_lb)], axis=1)
    mu = jnp.mean(y, axis=-1, keepdims=True)
    dlt = y - mu
    var = jnp.mean(dlt * dlt, axis=-1, keepdims=True)
    z = dlt * lax.rsqrt(var + EPS) * lg_ref[...] + lb_ref[...]
    z = (z * jax.nn.sigmoid(z)).astype(BF16)
    u2 = jnp.dot(pout_ref[...], z, preferred_element_type=F32).astype(BF16)

    br_b = jnp.dot(u2, wco_ref[...], preferred_element_type=F32)
    merged = gated_a + gate(half) * br_b
    o_ref[...] = x + jnp.dot(merged.astype(BF16), wo_ref[...], preferred_element_type=F32)


def _conv_layout_selectors(tm):
    run = tm // SUBLANES
    first = CONV_HALO - CONV_K // 2
    n_v = run + CONV_K - 1
    v, s = np.meshgrid(np.arange(n_v), np.arange(SUBLANES), indexing="ij")
    p_in = np.zeros((n_v * SUBLANES, tm + 2 * CONV_HALO), np.float32)
    p_in[(v * SUBLANES + s).ravel(), (first + v + s * run).ravel()] = 1.0
    v, s = np.meshgrid(np.arange(run), np.arange(SUBLANES), indexing="ij")
    p_out = np.zeros((tm, tm), np.float32)
    p_out[(s * run + v).ravel(), (v * SUBLANES + s).ravel()] = 1.0
    return jnp.asarray(p_in, BF16), jnp.asarray(p_out, BF16)


def _mixer(x2d, a2d, u2d, seq, g1, wg, bg, cw, cb, lg, lb, wna, wco, wo):
    m = x2d.shape[0]
    tm = TM_MIXER
    run = tm // SUBLANES
    assert seq % tm == 0 and run % BF16_SUBLANES == 0 and run % CONV_VREG_BLOCK == 0
    tiles_per_seq = seq // tm
    hb = tm // CONV_HALO
    n_hb = m // CONV_HALO
    n_lb = CONV_CH // LANES
    n_chunks = 2 * D_MODEL // GATE_CHUNK
    assert (n_lb * (run // CONV_VREG_BLOCK)) % n_chunks == 0
    wg = wg.reshape(D_MODEL, n_chunks, GATE_CHUNK).transpose(1, 0, 2)
    bg = bg.reshape(n_chunks, 1, GATE_CHUNK)
    cw = cw.reshape(CONV_K, n_lb, LANES).transpose(1, 0, 2)
    cb = cb.reshape(n_lb, 1, LANES)
    p_in, p_out = _conv_layout_selectors(tm)
    full = lambda arr: pl.BlockSpec(arr.shape, lambda i: (0,) * arr.ndim)
    return pl.pallas_call(
        functools.partial(_mixer_kernel, tiles_per_seq=tiles_per_seq),
        grid=(m // tm,),
        in_specs=[
            pl.BlockSpec((tm, D_MODEL), lambda i: (i, 0)),
            pl.BlockSpec((tm, D_ATTN), lambda i: (i, 0)),
            pl.BlockSpec((tm, CONV_CH), lambda i: (i, 0)),
            pl.BlockSpec((CONV_HALO, CONV_CH), lambda i: (jnp.maximum(i * hb - 1, 0), 0)),
            pl.BlockSpec((CONV_HALO, CONV_CH), lambda i: (jnp.minimum((i + 1) * hb, n_hb - 1), 0)),
            full(g1), full(wg), full(bg), full(cw), full(cb), full(lg), full(lb),
            full(wna), full(wco), full(wo), full(p_in), full(p_out),
        ],
        out_specs=pl.BlockSpec((tm, D_MODEL), lambda i: (i, 0)),
        out_shape=jax.ShapeDtypeStruct((m, D_MODEL), F32),
        scratch_shapes=[
            pltpu.VMEM((n_lb, p_in.shape[0], LANES), F32),
            pltpu.VMEM((n_lb, tm, LANES), F32),
            pltpu.VMEM((n_chunks, tm, GATE_CHUNK), F32),
            pltpu.VMEM((tm, D_MODEL), BF16),
        ],
        compiler_params=pltpu.CompilerParams(
            dimension_semantics=("arbitrary",), vmem_limit_bytes=VMEM_LIMIT_BYTES),
        name="mixer",
    )(x2d, a2d, u2d, u2d, u2d, g1, wg, bg, cw, cb, lg, lb, wna, wco, wo, p_in, p_out)


def _ffn_kernel(x_ref, xp_ref, xn_ref, g2_ref, wup_ref, cw_ref, cb_ref, wdn_ref, gf_ref, o_ref,
                act_ref, *, tiles_per_seq, final_norm):
    tm = x_ref.shape[0]
    t = pl.program_id(0) % tiles_per_seq
    not_first = (t != 0).astype(F32)
    not_last = (t != tiles_per_seq - 1).astype(F32)

    x = x_ref[...]
    g2 = g2_ref[...]
    h_ext = jnp.concatenate([
        (_rms_scale(xp_ref[...]) * g2 * not_first).astype(BF16),
        (_rms_scale(x) * g2).astype(BF16),
        (_rms_scale(xn_ref[...]) * g2 * not_last).astype(BF16),
    ], axis=0)

    def conv_up(col0):
        cols = slice(col0, col0 + FFN_CHUNK)
        up = jnp.dot(h_ext, wup_ref[:, cols], preferred_element_type=F32)
        y = cb_ref[:, cols]
        for kk in range(FFN_CONV_K):
            lo = FFN_HALO - FFN_CONV_K // 2 + kk
            y = y + cw_ref[kk:kk + 1, cols] * up[lo:lo + tm]
        return y

    for c in range(D_FF // FFN_CHUNK):
        gate = conv_up(c * FFN_CHUNK)
        val = conv_up(D_FF + c * FFN_CHUNK)
        act_ref[:, c * FFN_CHUNK:(c + 1) * FFN_CHUNK] = (_gelu_exact(gate) * val).astype(BF16)

    y = x + jnp.dot(act_ref[...], wdn_ref[...], preferred_element_type=F32)
    if final_norm:
        y = _rms_scale(y) * gf_ref[...]
    o_ref[...] = y


def _ffn(x2d, seq, g2, wup, cw, cb, wdn, gf, final_norm):
    m = x2d.shape[0]
    tm = TM_FFN
    assert seq % tm == 0
    tiles_per_seq = seq // tm
    hb = tm // FFN_HALO
    n_hb = m // FFN_HALO
    const = lambda i: (0, 0)
    full = lambda arr: pl.BlockSpec(arr.shape, const)
    resident = lambda arr: pl.BlockSpec(arr.shape, const, pipeline_mode=pl.Buffered(1))
    return pl.pallas_call(
        functools.partial(_ffn_kernel, tiles_per_seq=tiles_per_seq, final_norm=final_norm),
        grid=(m // tm,),
        in_specs=[
            pl.BlockSpec((tm, D_MODEL), lambda i: (i, 0)),
            pl.BlockSpec((FFN_HALO, D_MODEL), lambda i: (jnp.maximum(i * hb - 1, 0), 0)),
            pl.BlockSpec((FFN_HALO, D_MODEL), lambda i: (jnp.minimum((i + 1) * hb, n_hb - 1), 0)),
            full(g2), resident(wup), full(cw), full(cb), resident(wdn), full(gf),
        ],
        out_specs=pl.BlockSpec((tm, D_MODEL), lambda i: (i, 0)),
        out_shape=jax.ShapeDtypeStruct((m, D_MODEL), F32),
        scratch_shapes=[pltpu.VMEM((tm, D_FF), BF16)],
        compiler_params=pltpu.CompilerParams(
            dimension_semantics=("arbitrary",), vmem_limit_bytes=VMEM_LIMIT_BYTES),
        name="ffn",
    )(x2d, x2d, x2d, g2, wup, cw, cb, wdn, gf)


def kernel(x, norm1_g, w_in, b_in, rpb, w_na_out, conv_dw_w, conv_dw_b, conv_ln_g, conv_ln_b,
           w_conv_out, w_out, norm2_g, w_up, ffn_dw_w, ffn_dw_b, w_down, norm_f_g):
    bsz, seq, d = x.shape
    depth = w_in.shape[0]
    n_tok = 3 * D_ATTN + 2 * CONV_CH
    row = lambda v: v.reshape(1, -1).astype(F32)
    x2d = x.reshape(bsz * seq, d)
    for l in range(depth):
        w_in_l = w_in[l].astype(BF16)
        b_in_l = row(b_in[l])
        g1 = row(norm1_g[l])
        q, k, v, u = _inproj(x2d, g1, w_in_l[:, :n_tok], b_in_l[:, :n_tok])
        shp = (bsz, seq, D_ATTN)
        a = _natten(q.reshape(shp), k.reshape(shp), v.reshape(shp), _natten_bias_table(rpb[l]))
        x2d = _mixer(
            x2d, a.reshape(bsz * seq, D_ATTN), u, seq, g1, w_in_l[:, n_tok:], b_in_l[:, n_tok:],
            conv_dw_w[l].astype(F32), row(conv_dw_b[l]), row(conv_ln_g[l]), row(conv_ln_b[l]),
            w_na_out[l].astype(BF16), w_conv_out[l].astype(BF16), w_out[l].astype(BF16))
        x2d = _ffn(
            x2d, seq, row(norm2_g[l]), w_up[l].astype(BF16), ffn_dw_w[l].astype(F32),
            row(ffn_dw_b[l]), w_down[l].astype(BF16), row(norm_f_g), final_norm=(l == depth - 1))
    return x2d.reshape(bsz, seq, d)
```

```python
import functools

import jax
import jax.numpy as jnp
import numpy as np
from jax import lax
from jax.experimental import pallas as pl
from jax.experimental.pallas import tpu as pltpu

F32 = jnp.float32
BF16 = jnp.bfloat16

D_MODEL = 1024
GRID_W = 64
NA_HEADS = 8
NA_HEAD_DIM = 64
D_ATTN = NA_HEADS * NA_HEAD_DIM
NA_KH = 8
NA_KW = 16
CONV_CH = 512
CONV_K = 31
D_FF = 2816
FFN_CONV_K = 3
EPS = 1e-6
NEG = -1e30
LOG2_E = 1.4426950408889634

LANES = 128
SUBLANES = 8
BF16_SUBLANES = 16
VMEM_LIMIT_BYTES = 56 * 1024 * 1024

TM_INPROJ = 512
TM_MIXER = 512
TM_FFN = 512
NATTEN_ROWS_PER_STEP = 8
NATTEN_ROWS_PER_ITER = 4
CONV_VREG_BLOCK = 8
GATE_CHUNK = 512
CONV_HALO = 16
FFN_HALO = 8
FFN_CHUNK = 256


def _rms_scale(x):
    return x * lax.rsqrt(jnp.mean(x * x, axis=-1, keepdims=True) + EPS)


def _gelu_exact(x):
    return 0.5 * x * (1.0 + lax.erf(x * np.float32(np.sqrt(0.5))))


def _inproj_kernel(x_ref, g_ref, w_ref, b_ref, q_ref, k_ref, v_ref, u_ref):
    h = (_rms_scale(x_ref[...]) * g_ref[...]).astype(BF16)

    def proj(n):
        cols = slice(n * D_ATTN, (n + 1) * D_ATTN)
        return jnp.dot(h, w_ref[:, cols], preferred_element_type=F32) + b_ref[:, cols]

    q_ref[...] = (proj(0) * (NA_HEAD_DIM ** -0.5 * LOG2_E)).astype(BF16)
    k_ref[...] = proj(1).astype(BF16)
    v_ref[...] = proj(2).astype(BF16)
    u_ref[...] = (proj(3) * jax.nn.sigmoid(proj(4))).astype(BF16)


def _inproj(x2d, g, w, b):
    m = x2d.shape[0]
    tm = TM_INPROJ
    n_in = w.shape[1]
    const = lambda i: (0, 0)
    out = jax.ShapeDtypeStruct((m, D_ATTN), BF16)
    row_spec = pl.BlockSpec((tm, D_ATTN), lambda i: (i, 0))
    return pl.pallas_call(
        _inproj_kernel,
        grid=(m // tm,),
        in_specs=[
            pl.BlockSpec((tm, D_MODEL), lambda i: (i, 0)),
            pl.BlockSpec((1, D_MODEL), const),
            pl.BlockSpec((D_MODEL, n_in), const),
            pl.BlockSpec((1, n_in), const),
        ],
        out_specs=[row_spec, row_spec, row_spec, row_spec],
        out_shape=[out, out, out, out],
        compiler_params=pltpu.CompilerParams(
            dimension_semantics=("arbitrary",), vmem_limit_bytes=VMEM_LIMIT_BYTES),
        name="inproj",
    )(x2d, g, w, b)


def _natten_bias_table(rpb):
    col = jnp.arange(GRID_W)
    col_start = jnp.clip(col - NA_KW // 2, 0, GRID_W - NA_KW)
    kc = col[None, :]
    in_win = (kc >= col_start[:, None]) & (kc < col_start[:, None] + NA_KW)
    reach = GRID_W - NA_KW
    padded = jnp.pad(rpb.astype(F32) * LOG2_E, ((0, 0), (0, 0), (reach, reach)))
    t = jnp.stack([padded[:, :, GRID_W - 1 - c:2 * GRID_W - 1 - c] for c in range(GRID_W)], axis=2)
    t = jnp.where(in_win[None, None], t, NEG)
    nxt = jnp.concatenate([t[:, 1:], jnp.full_like(t[:, :1], NEG)], axis=1)
    return jnp.concatenate([t, nxt], axis=-1)


def _natten_kernel(q_ref, k_ref, v_ref, tb_ref, o_ref, *, rows):
    step = pl.program_id(1)
    lane = lax.broadcasted_iota(jnp.int32, (GRID_W, LANES), 1)
    lo_half = lane < NA_HEAD_DIM
    n_keys = NA_KH * GRID_W

    def row_group(it, carry):
        jobs = []
        for sub in range(NATTEN_ROWS_PER_ITER):
            rr = it * NATTEN_ROWS_PER_ITER + sub
            r = step * NATTEN_ROWS_PER_STEP + rr
            rs = jnp.clip(r - NA_KH // 2, 0, rows - NA_KH)
            base = rs - r + NA_KH - 1
            q0 = pl.multiple_of(rr * GRID_W, GRID_W)
            k0 = pl.multiple_of(rs * GRID_W, GRID_W)
            jobs += [(q0, k0, base, p) for p in range(NA_HEADS // 2)]

        scores = []
        for q0, k0, base, p in jobs:
            cols = slice(p * LANES, (p + 1) * LANES)
            qp = q_ref[0, pl.ds(q0, GRID_W), cols]
            kp = k_ref[0, pl.ds(k0, n_keys), cols]
            zero = jnp.zeros_like(qp)
            q2 = jnp.concatenate([jnp.where(lo_half, qp, zero), jnp.where(lo_half, zero, qp)], axis=0)
            s = lax.dot_general(q2, kp, (((1,), (1,)), ((), ())), preferred_element_type=F32)
            bias = jnp.concatenate(
                [jnp.concatenate([tb_ref[2 * p + hh, base + 2 * j] for j in range(NA_KH // 2)], axis=1)
                 for hh in range(2)], axis=0)
            scores.append(s + bias)

        probs = []
        for s in scores:
            e = jnp.exp2(s - jnp.max(s, axis=-1, keepdims=True))
            probs.append((e.astype(BF16), jnp.sum(e, axis=-1, keepdims=True)))

        outs = []
        for (e, l), (q0, k0, base, p) in zip(probs, jobs):
            vp = v_ref[0, pl.ds(k0, n_keys), p * LANES:(p + 1) * LANES]
            o2 = jnp.dot(e, vp, preferred_element_type=F32) / l
            outs.append(jnp.where(lo_half, o2[:GRID_W], o2[GRID_W:]))

        pairs = NA_HEADS // 2
        for sub in range(NATTEN_ROWS_PER_ITER):
            q0 = jobs[sub * pairs][0]
            o_ref[0, pl.ds(q0, GRID_W), :] = jnp.concatenate(
                outs[sub * pairs:(sub + 1) * pairs], axis=1).astype(o_ref.dtype)
        return carry

    lax.fori_loop(0, NATTEN_ROWS_PER_STEP // NATTEN_ROWS_PER_ITER, row_group, 0)


def _natten(q, k, v, table):
    bsz, s, _ = q.shape
    rows = s // GRID_W
    assert rows >= NA_KH and rows % NATTEN_ROWS_PER_STEP == 0
    tq = NATTEN_ROWS_PER_STEP * GRID_W
    return pl.pallas_call(
        functools.partial(_natten_kernel, rows=rows),
        grid=(bsz, rows // NATTEN_ROWS_PER_STEP),
        in_specs=[
            pl.BlockSpec((1, tq, D_ATTN), lambda b, i: (b, i, 0)),
            pl.BlockSpec((1, s, D_ATTN), lambda b, i: (b, 0, 0)),
            pl.BlockSpec((1, s, D_ATTN), lambda b, i: (b, 0, 0)),
            pl.BlockSpec(table.shape, lambda b, i: (0, 0, 0, 0)),
        ],
        out_specs=pl.BlockSpec((1, tq, D_ATTN), lambda b, i: (b, i, 0)),
        out_shape=jax.ShapeDtypeStruct((bsz, s, D_ATTN), BF16),
        compiler_params=pltpu.CompilerParams(
            dimension_semantics=("arbitrary", "arbitrary"), vmem_limit_bytes=VMEM_LIMIT_BYTES),
        name="natten",
    )(q, k, v, table)


def _mixer_kernel(x_ref, a_ref, u_ref, up_ref, un_ref, g1_ref, wg_ref, bg_ref, cw_ref, cb_ref,
                  lg_ref, lb_ref, wna_ref, wco_ref, wo_ref, pin_ref, o_ref, d_ref, y_ref, u2_ref,
                  gates_ref, h_ref, *, tiles_per_seq):
    tm = x_ref.shape[0]
    run = tm // SUBLANES
    pad = CONV_K // 2
    n_lb = CONV_CH // LANES
    t = pl.program_id(0) % tiles_per_seq
    prev = up_ref[...].astype(F32) * (t != 0).astype(F32)
    nxt = un_ref[...].astype(F32) * (t != tiles_per_seq - 1).astype(F32)

    d_main = jnp.dot(pin_ref[...], u_ref[...], preferred_element_type=F32)
    sub = lax.broadcasted_iota(jnp.int32, (SUBLANES, LANES), 0)
    vreg = lambda v, cols: d_main[v * SUBLANES:(v + 1) * SUBLANES, cols]
    for lb in range(n_lb):
        cols = slice(lb * LANES, (lb + 1) * LANES)
        d_ref[lb, pad * SUBLANES:pad * SUBLANES + tm, :] = d_main[:, cols]
        for j in range(pad):
            lo = jnp.where(sub == 0, prev[CONV_HALO - pad + j:CONV_HALO - pad + j + 1, cols],
                           pltpu.roll(vreg(run - pad + j, cols), 1, axis=0))
            d_ref[lb, j * SUBLANES:(j + 1) * SUBLANES, :] = lo
            hi = jnp.where(sub == SUBLANES - 1, nxt[j:j + 1, cols],
                           pltpu.roll(vreg(j, cols), SUBLANES - 1, axis=0))
            d_ref[lb, (pad + run + j) * SUBLANES:(pad + run + j + 1) * SUBLANES, :] = hi

    x = x_ref[...]
    h_ref[...] = (_rms_scale(x) * g1_ref[...]).astype(BF16)

    n_acc = CONV_VREG_BLOCK
    blocks_per_lb = run // n_acc
    n_chunks = gates_ref.shape[0]
    blocks_per_chunk = n_lb * blocks_per_lb // n_chunks

    def conv_block(blk):
        lb = blk // blocks_per_lb
        row0 = pl.multiple_of((blk % blocks_per_lb) * (n_acc * SUBLANES), n_acc * SUBLANES)
        w = [cw_ref[lb, kk:kk + 1, :] for kk in range(CONV_K)]
        accs = [cb_ref[lb] + jnp.zeros((SUBLANES, LANES), F32)] * n_acc
        for i in range(n_acc + CONV_K - 1):
            dv = d_ref[lb, pl.ds(row0 + i * SUBLANES, SUBLANES), :]
            for j in range(n_acc):
                if 0 <= i - j < CONV_K:
                    accs[j] = accs[j] + w[i - j] * dv
        return lb, row0, accs

    def chunk(c, carry):
        convs = [conv_block(c * blocks_per_chunk + b) for b in range(blocks_per_chunk)]
        g = jnp.dot(h_ref[...], wg_ref[c], preferred_element_type=F32) + bg_ref[c]
        for lb, row0, accs in convs:
            for j in range(n_acc):
                y_ref[lb, pl.ds(row0 + j * SUBLANES, SUBLANES), :] = accs[j]
        gates_ref[c] = g
        return carry

    lax.fori_loop(0, n_chunks, chunk, 0)

    half = n_chunks // 2
    gate = lambda first: jax.nn.sigmoid(
        jnp.concatenate([gates_ref[first + n] for n in range(half)], axis=1))
    gated_a = gate(0) * jnp.dot(a_ref[...], wna_ref[...], preferred_element_type=F32)

    y = jnp.concatenate([y_ref[lb] for lb in range(n_lb)], axis=1)
    mu = jnp.mean(y, axis=-1, keepdims=True)
    dlt = y - mu
    var = jnp.mean(dlt * dlt, axis=-1, keepdims=True)
    z = dlt * lax.rsqrt(var + EPS) * lg_ref[...] + lb_ref[...]
    z = z * jax.nn.sigmoid(z)
    for lb in range(n_lb):
        y_ref[lb] = z[:, lb * LANES:(lb + 1) * LANES]
    for lb in range(n_lb):
        for s in range(SUBLANES):
            for v0 in range(0, run, BF16_SUBLANES):
                u2_ref[s * run + v0:s * run + v0 + BF16_SUBLANES, lb * LANES:(lb + 1) * LANES] = (
                    y_ref[lb, pl.ds(v0 * SUBLANES + s, BF16_SUBLANES, stride=SUBLANES), :].astype(BF16))

    br_b = jnp.dot(u2_ref[...], wco_ref[...], preferred_element_type=F32)
    merged = gated_a + gate(half) * br_b
    o_ref[...] = x + jnp.dot(merged.astype(BF16), wo_ref[...], preferred_element_type=F32)


def _conv_layout_selector(tm):
    run = tm // SUBLANES
    v, s = np.meshgrid(np.arange(run), np.arange(SUBLANES), indexing="ij")
    p_in = np.zeros((tm, tm), np.float32)
    p_in[(v * SUBLANES + s).ravel(), (s * run + v).ravel()] = 1.0
    return jnp.asarray(p_in, BF16)


def _mixer(x2d, a2d, u2d, seq, g1, wg, bg, cw, cb, lg, lb, wna, wco, wo):
    m = x2d.shape[0]
    tm = TM_MIXER
    run = tm // SUBLANES
    assert seq % tm == 0 and run % BF16_SUBLANES == 0 and run % CONV_VREG_BLOCK == 0
    tiles_per_seq = seq // tm
    hb = tm // CONV_HALO
    n_hb = m // CONV_HALO
    n_lb = CONV_CH // LANES
    n_chunks = 2 * D_MODEL // GATE_CHUNK
    assert (n_lb * (run // CONV_VREG_BLOCK)) % n_chunks == 0
    wg = wg.reshape(D_MODEL, n_chunks, GATE_CHUNK).transpose(1, 0, 2)
    bg = bg.reshape(n_chunks, 1, GATE_CHUNK)
    cw = cw.reshape(CONV_K, n_lb, LANES).transpose(1, 0, 2)
    cb = cb.reshape(n_lb, 1, LANES)
    p_in = _conv_layout_selector(tm)
    full = lambda arr: pl.BlockSpec(arr.shape, lambda i: (0,) * arr.ndim)
    return pl.pallas_call(
        functools.partial(_mixer_kernel, tiles_per_seq=tiles_per_seq),
        grid=(m // tm,),
        in_specs=[
            pl.BlockSpec((tm, D_MODEL), lambda i: (i, 0)),
            pl.BlockSpec((tm, D_ATTN), lambda i: (i, 0)),
            pl.BlockSpec((tm, CONV_CH), lambda i: (i, 0)),
            pl.BlockSpec((CONV_HALO, CONV_CH), lambda i: (jnp.maximum(i * hb - 1, 0), 0)),
            pl.BlockSpec((CONV_HALO, CONV_CH), lambda i: (jnp.minimum((i + 1) * hb, n_hb - 1), 0)),
            full(g1), full(wg), full(bg), full(cw), full(cb), full(lg), full(lb),
            full(wna), full(wco), full(wo), full(p_in),
        ],
        out_specs=pl.BlockSpec((tm, D_MODEL), lambda i: (i, 0)),
        out_shape=jax.ShapeDtypeStruct((m, D_MODEL), F32),
        scratch_shapes=[
            pltpu.VMEM((n_lb, tm + 2 * (CONV_K // 2) * SUBLANES, LANES), F32),
            pltpu.VMEM((n_lb, tm, LANES), F32),
            pltpu.VMEM((tm, CONV_CH), BF16),
            pltpu.VMEM((n_chunks, tm, GATE_CHUNK), F32),
            pltpu.VMEM((tm, D_MODEL), BF16),
        ],
        compiler_params=pltpu.CompilerParams(
            dimension_semantics=("arbitrary",), vmem_limit_bytes=VMEM_LIMIT_BYTES),
        name="mixer",
    )(x2d, a2d, u2d, u2d, u2d, g1, wg, bg, cw, cb, lg, lb, wna, wco, wo, p_in)


def _ffn_kernel(x_ref, xp_ref, xn_ref, g2_ref, wup_ref, cw_ref, cb_ref, wdn_ref, gf_ref, o_ref,
                act_ref, *, tiles_per_seq, final_norm):
    tm = x_ref.shape[0]
    t = pl.program_id(0) % tiles_per_seq
    not_first = (t != 0).astype(F32)
    not_last = (t != tiles_per_seq - 1).astype(F32)

    x = x_ref[...]
    g2 = g2_ref[...]
    h_ext = jnp.concatenate([
        (_rms_scale(xp_ref[...]) * g2 * not_first).astype(BF16),
        (_rms_scale(x) * g2).astype(BF16),
        (_rms_scale(xn_ref[...]) * g2 * not_last).astype(BF16),
    ], axis=0)

    def conv_up(col0):
        cols = slice(col0, col0 + FFN_CHUNK)
        up = jnp.dot(h_ext, wup_ref[:, cols], preferred_element_type=F32)
        y = cb_ref[:, cols]
        for kk in range(FFN_CONV_K):
            lo = FFN_HALO - FFN_CONV_K // 2 + kk
            y = y + cw_ref[kk:kk + 1, cols] * up[lo:lo + tm]
        return y

    for c in range(D_FF // FFN_CHUNK):
        gate = conv_up(c * FFN_CHUNK)
        val = conv_up(D_FF + c * FFN_CHUNK)
        act_ref[:, c * FFN_CHUNK:(c + 1) * FFN_CHUNK] = (_gelu_exact(gate) * val).astype(BF16)

    y = x + jnp.dot(act_ref[...], wdn_ref[...], preferred_element_type=F32)
    if final_norm:
        y = _rms_scale(y) * gf_ref[...]
    o_ref[...] = y


def _ffn(x2d, seq, g2, wup, cw, cb, wdn, gf, final_norm):
    m = x2d.shape[0]
    tm = TM_FFN
    assert seq % tm == 0
    tiles_per_seq = seq // tm
    hb = tm // FFN_HALO
    n_hb = m // FFN_HALO
    const = lambda i: (0, 0)
    full = lambda arr: pl.BlockSpec(arr.shape, const)
    resident = lambda arr: pl.BlockSpec(arr.shape, const, pipeline_mode=pl.Buffered(1))
    return pl.pallas_call(
        functools.partial(_ffn_kernel, tiles_per_seq=tiles_per_seq, final_norm=final_norm),
        grid=(m // tm,),
        in_specs=[
            pl.BlockSpec((tm, D_MODEL), lambda i: (i, 0)),
            pl.BlockSpec((FFN_HALO, D_MODEL), lambda i: (jnp.maximum(i * hb - 1, 0), 0)),
            pl.BlockSpec((FFN_HALO, D_MODEL), lambda i: (jnp.minimum((i + 1) * hb, n_hb - 1), 0)),
            full(g2), resident(wup), full(cw), full(cb), resident(wdn), full(gf),
        ],
        out_specs=pl.BlockSpec((tm, D_MODEL), lambda i: (i, 0)),
        out_shape=jax.ShapeDtypeStruct((m, D_MODEL), F32),
        scratch_shapes=[pltpu.VMEM((tm, D_FF), BF16)],
        compiler_params=pltpu.CompilerParams(
            dimension_semantics=("arbitrary",), vmem_limit_bytes=VMEM_LIMIT_BYTES),
        name="ffn",
    )(x2d, x2d, x2d, g2, wup, cw, cb, wdn, gf)


def kernel(x, norm1_g, w_in, b_in, rpb, w_na_out, conv_dw_w, conv_dw_b, conv_ln_g, conv_ln_b,
           w_conv_out, w_out, norm2_g, w_up, ffn_dw_w, ffn_dw_b, w_down, norm_f_g):
    bsz, seq, d = x.shape
    depth = w_in.shape[0]
    n_tok = 3 * D_ATTN + 2 * CONV_CH
    row = lambda v: v.reshape(1, -1).astype(F32)
    x2d = x.reshape(bsz * seq, d)
    for l in range(depth):
        w_in_l = w_in[l].astype(BF16)
        b_in_l = row(b_in[l])
        g1 = row(norm1_g[l])
        q, k, v, u = _inproj(x2d, g1, w_in_l[:, :n_tok], b_in_l[:, :n_tok])
        shp = (bsz, seq, D_ATTN)
        a = _natten(q.reshape(shp), k.reshape(shp), v.reshape(shp), _natten_bias_table(rpb[l]))
        x2d = _mixer(
            x2d, a.reshape(bsz * seq, D_ATTN), u, seq, g1, w_in_l[:, n_tok:], b_in_l[:, n_tok:],
            conv_dw_w[l].astype(F32), row(conv_dw_b[l]), row(conv_ln_g[l]), row(conv_ln_b[l]),
            w_na_out[l].astype(BF16), w_conv_out[l].astype(BF16), w_out[l].astype(BF16))
        x2d = _ffn(
            x2d, seq, row(norm2_g[l]), w_up[l].astype(BF16), ffn_dw_w[l].astype(F32),
            row(ffn_dw_b[l]), w_down[l].astype(BF16), row(norm_f_g), final_norm=(l == depth - 1))
    return x2d.reshape(bsz, seq, d)
```

```python
import functools

import jax
import jax.numpy as jnp
import numpy as np
from jax import lax
from jax.experimental import pallas as pl
from jax.experimental.pallas import tpu as pltpu

F32 = jnp.float32
BF16 = jnp.bfloat16

D_MODEL = 1024
GRID_W = 64
NA_HEADS = 8
NA_HEAD_DIM = 64
D_ATTN = NA_HEADS * NA_HEAD_DIM
NA_KH = 8
NA_KW = 16
CONV_CH = 512
CONV_K = 31
D_FF = 2816
FFN_CONV_K = 3
EPS = 1e-6
NEG = -1e30
LOG2_E = 1.4426950408889634

LANES = 128
SUBLANES = 8
BF16_SUBLANES = 16
VMEM_LIMIT_BYTES = 56 * 1024 * 1024

TM_INPROJ = 1024
TM_MIXER = 512
TM_FFN = 512
NATTEN_ROWS_PER_STEP = 8
NATTEN_ROWS_PER_ITER = 4
CONV_VREG_BLOCK = 8
GATE_CHUNK = 512
CONV_HALO = 16
FFN_HALO = 8
FFN_CHUNK = 256


def _rms_scale(x):
    return x * lax.rsqrt(jnp.mean(x * x, axis=-1, keepdims=True) + EPS)


def _twice_gelu_exact(x):
    return x * (1.0 + lax.erf(x * np.float32(np.sqrt(0.5))))


def _inproj_kernel(x_ref, g_ref, w_ref, b_ref, q_ref, k_ref, v_ref, u_ref):
    h = (_rms_scale(x_ref[...]) * g_ref[...]).astype(BF16)

    def proj(n):
        cols = slice(n * D_ATTN, (n + 1) * D_ATTN)
        return jnp.dot(h, w_ref[:, cols], preferred_element_type=F32) + b_ref[:, cols]

    q_ref[...] = (proj(0) * (NA_HEAD_DIM ** -0.5 * LOG2_E)).astype(BF16)
    k_ref[...] = proj(1).astype(BF16)
    v_ref[...] = proj(2).astype(BF16)
    u_ref[...] = (proj(3) * jax.nn.sigmoid(proj(4))).astype(BF16)


def _inproj(x2d, g, w, b):
    m = x2d.shape[0]
    tm = TM_INPROJ
    n_in = w.shape[1]
    const = lambda i: (0, 0)
    out = jax.ShapeDtypeStruct((m, D_ATTN), BF16)
    row_spec = pl.BlockSpec((tm, D_ATTN), lambda i: (i, 0))
    return pl.pallas_call(
        _inproj_kernel,
        grid=(m // tm,),
        in_specs=[
            pl.BlockSpec((tm, D_MODEL), lambda i: (i, 0)),
            pl.BlockSpec((1, D_MODEL), const),
            pl.BlockSpec((D_MODEL, n_in), const),
            pl.BlockSpec((1, n_in), const),
        ],
        out_specs=[row_spec, row_spec, row_spec, row_spec],
        out_shape=[out, out, out, out],
        compiler_params=pltpu.CompilerParams(
            dimension_semantics=("arbitrary",), vmem_limit_bytes=VMEM_LIMIT_BYTES),
        name="inproj",
    )(x2d, g, w, b)


def _natten_bias_table(rpb):
    col = jnp.arange(GRID_W)
    col_start = jnp.clip(col - NA_KW // 2, 0, GRID_W - NA_KW)
    kc = col[None, :]
    in_win = (kc >= col_start[:, None]) & (kc < col_start[:, None] + NA_KW)
    reach = GRID_W - NA_KW
    padded = jnp.pad(rpb.astype(F32) * LOG2_E, ((0, 0), (0, 0), (reach, reach)))
    t = jnp.stack([padded[:, :, GRID_W - 1 - c:2 * GRID_W - 1 - c] for c in range(GRID_W)], axis=2)
    t = jnp.where(in_win[None, None], t, NEG)
    nxt = jnp.concatenate([t[:, 1:], jnp.full_like(t[:, :1], NEG)], axis=1)
    return jnp.concatenate([t, nxt], axis=-1)


def _natten_kernel(q_ref, k_ref, v_ref, tb_ref, o_ref, *, rows):
    step = pl.program_id(1)
    lane = lax.broadcasted_iota(jnp.int32, (GRID_W, LANES), 1)
    lo_half = lane < NA_HEAD_DIM
    n_keys = NA_KH * GRID_W

    def row_group(it, carry):
        jobs = []
        for sub in range(NATTEN_ROWS_PER_ITER):
            rr = it * NATTEN_ROWS_PER_ITER + sub
            r = step * NATTEN_ROWS_PER_STEP + rr
            rs = jnp.clip(r - NA_KH // 2, 0, rows - NA_KH)
            base = rs - r + NA_KH - 1
            q0 = pl.multiple_of(rr * GRID_W, GRID_W)
            k0 = pl.multiple_of(rs * GRID_W, GRID_W)
            jobs += [(q0, k0, base, p) for p in range(NA_HEADS // 2)]

        scores = []
        for q0, k0, base, p in jobs:
            cols = slice(p * LANES, (p + 1) * LANES)
            qp = q_ref[0, pl.ds(q0, GRID_W), cols]
            kp = k_ref[0, pl.ds(k0, n_keys), cols]
            zero = jnp.zeros_like(qp)
            q2 = jnp.concatenate([jnp.where(lo_half, qp, zero), jnp.where(lo_half, zero, qp)], axis=0)
            s = lax.dot_general(q2, kp, (((1,), (1,)), ((), ())), preferred_element_type=F32)
            bias = jnp.concatenate(
                [jnp.concatenate([tb_ref[2 * p + hh, base + 2 * j] for j in range(NA_KH // 2)], axis=1)
                 for hh in range(2)], axis=0)
            scores.append(s + bias)

        probs = []
        for s in scores:
            e = jnp.exp2(s - jnp.max(s, axis=-1, keepdims=True))
            probs.append((e.astype(BF16), jnp.sum(e, axis=-1, keepdims=True)))

        outs = []
        for (e, l), (q0, k0, base, p) in zip(probs, jobs):
            vp = v_ref[0, pl.ds(k0, n_keys), p * LANES:(p + 1) * LANES]
            o2 = jnp.dot(e, vp, preferred_element_type=F32) / l
            outs.append(jnp.where(lo_half, o2[:GRID_W], o2[GRID_W:]))

        pairs = NA_HEADS // 2
        for sub in range(NATTEN_ROWS_PER_ITER):
            q0 = jobs[sub * pairs][0]
            o_ref[0, pl.ds(q0, GRID_W), :] = jnp.concatenate(
                outs[sub * pairs:(sub + 1) * pairs], axis=1).astype(o_ref.dtype)
        return carry

    lax.fori_loop(0, NATTEN_ROWS_PER_STEP // NATTEN_ROWS_PER_ITER, row_group, 0)


def _natten(q, k, v, table):
    bsz, s, _ = q.shape
    rows = s // GRID_W
    assert rows >= NA_KH and rows % NATTEN_ROWS_PER_STEP == 0
    tq = NATTEN_ROWS_PER_STEP * GRID_W
    return pl.pallas_call(
        functools.partial(_natten_kernel, rows=rows),
        grid=(bsz, rows // NATTEN_ROWS_PER_STEP),
        in_specs=[
            pl.BlockSpec((1, tq, D_ATTN), lambda b, i: (b, i, 0)),
            pl.BlockSpec((1, s, D_ATTN), lambda b, i: (b, 0, 0)),
            pl.BlockSpec((1, s, D_ATTN), lambda b, i: (b, 0, 0)),
            pl.BlockSpec(table.shape, lambda b, i: (0, 0, 0, 0)),
        ],
        out_specs=pl.BlockSpec((1, tq, D_ATTN), lambda b, i: (b, i, 0)),
        out_shape=jax.ShapeDtypeStruct((bsz, s, D_ATTN), BF16),
        compiler_params=pltpu.CompilerParams(
            dimension_semantics=("arbitrary", "arbitrary"), vmem_limit_bytes=VMEM_LIMIT_BYTES),
        name="natten",
    )(q, k, v, table)


def _mixer_kernel(x_ref, a_ref, u_ref, up_ref, un_ref, g1_ref, wg_ref, bg_ref, cw_ref, cb_ref,
                  lg_ref, lb_ref, wna_ref, wco_ref, wo_ref, pin_ref, o_ref, d_ref, y_ref, u2_ref,
                  gates_ref, h_ref, *, tiles_per_seq):
    tm = x_ref.shape[0]
    run = tm // SUBLANES
    pad = CONV_K // 2
    n_lb = CONV_CH // LANES
    t = pl.program_id(0) % tiles_per_seq
    prev = up_ref[...].astype(F32) * (t != 0).astype(F32)
    nxt = un_ref[...].astype(F32) * (t != tiles_per_seq - 1).astype(F32)

    d_main = jnp.dot(pin_ref[...], u_ref[...], preferred_element_type=F32)
    sub = lax.broadcasted_iota(jnp.int32, (SUBLANES, LANES), 0)
    vreg = lambda v, cols: d_main[v * SUBLANES:(v + 1) * SUBLANES, cols]
    for lb in range(n_lb):
        cols = slice(lb * LANES, (lb + 1) * LANES)
        d_ref[lb, pad * SUBLANES:pad * SUBLANES + tm, :] = d_main[:, cols]
        for j in range(pad):
            lo = jnp.where(sub == 0, prev[CONV_HALO - pad + j:CONV_HALO - pad + j + 1, cols],
                           pltpu.roll(vreg(run - pad + j, cols), 1, axis=0))
            d_ref[lb, j * SUBLANES:(j + 1) * SUBLANES, :] = lo
            hi = jnp.where(sub == SUBLANES - 1, nxt[j:j + 1, cols],
                           pltpu.roll(vreg(j, cols), SUBLANES - 1, axis=0))
            d_ref[lb, (pad + run + j) * SUBLANES:(pad + run + j + 1) * SUBLANES, :] = hi

    x = x_ref[...]
    h_ref[...] = (_rms_scale(x) * g1_ref[...]).astype(BF16)

    n_acc = CONV_VREG_BLOCK
    blocks_per_lb = run // n_acc
    n_chunks = gates_ref.shape[0]
    blocks_per_chunk = n_lb * blocks_per_lb // n_chunks

    def conv_block(blk):
        lb = blk // blocks_per_lb
        row0 = pl.multiple_of((blk % blocks_per_lb) * (n_acc * SUBLANES), n_acc * SUBLANES)
        w = [cw_ref[lb, kk:kk + 1, :] for kk in range(CONV_K)]
        accs = [cb_ref[lb] + jnp.zeros((SUBLANES, LANES), F32)] * n_acc
        for i in range(n_acc + CONV_K - 1):
            dv = d_ref[lb, pl.ds(row0 + i * SUBLANES, SUBLANES), :]
            for j in range(n_acc):
                if 0 <= i - j < CONV_K:
                    accs[j] = accs[j] + w[i - j] * dv
        return lb, row0, accs

    def chunk(c, carry):
        convs = [conv_block(c * blocks_per_chunk + b) for b in range(blocks_per_chunk)]
        g = jnp.dot(h_ref[...], wg_ref[c], preferred_element_type=F32) + bg_ref[c]
        for lb, row0, accs in convs:
            for j in range(n_acc):
                y_ref[lb, pl.ds(row0 + j * SUBLANES, SUBLANES), :] = accs[j]
        gates_ref[c] = g
        return carry

    lax.fori_loop(0, n_chunks, chunk, 0)

    half = n_chunks // 2
    gate = lambda first: jax.nn.sigmoid(
        jnp.concatenate([gates_ref[first + n] for n in range(half)], axis=1))
    gated_a = gate(0) * jnp.dot(a_ref[...], wna_ref[...], preferred_element_type=F32)

    y = jnp.concatenate([y_ref[lb] for lb in range(n_lb)], axis=1)
    mu = jnp.mean(y, axis=-1, keepdims=True)
    dlt = y - mu
    var = jnp.mean(dlt * dlt, axis=-1, keepdims=True)
    z = dlt * lax.rsqrt(var + EPS) * lg_ref[...] + lb_ref[...]
    z = z * jax.nn.sigmoid(z)
    for lb in range(n_lb):
        y_ref[lb] = z[:, lb * LANES:(lb + 1) * LANES]
    for lb in range(n_lb):
        for s in range(SUBLANES):
            for v0 in range(0, run, BF16_SUBLANES):
                u2_ref[s * run + v0:s * run + v0 + BF16_SUBLANES, lb * LANES:(lb + 1) * LANES] = (
                    y_ref[lb, pl.ds(v0 * SUBLANES + s, BF16_SUBLANES, stride=SUBLANES), :].astype(BF16))

    br_b = jnp.dot(u2_ref[...], wco_ref[...], preferred_element_type=F32)
    merged = gated_a + gate(half) * br_b
    o_ref[...] = x + jnp.dot(merged.astype(BF16), wo_ref[...], preferred_element_type=F32)


def _conv_layout_selector(tm):
    run = tm // SUBLANES
    v, s = np.meshgrid(np.arange(run), np.arange(SUBLANES), indexing="ij")
    p_in = np.zeros((tm, tm), np.float32)
    p_in[(v * SUBLANES + s).ravel(), (s * run + v).ravel()] = 1.0
    return jnp.asarray(p_in, BF16)


def _mixer(x2d, a2d, u2d, seq, g1, wg, bg, cw, cb, lg, lb, wna, wco, wo):
    m = x2d.shape[0]
    tm = TM_MIXER
    run = tm // SUBLANES
    assert seq % tm == 0 and run % BF16_SUBLANES == 0 and run % CONV_VREG_BLOCK == 0
    tiles_per_seq = seq // tm
    hb = tm // CONV_HALO
    n_hb = m // CONV_HALO
    n_lb = CONV_CH // LANES
    n_chunks = 2 * D_MODEL // GATE_CHUNK
    assert (n_lb * (run // CONV_VREG_BLOCK)) % n_chunks == 0
    wg = wg.reshape(D_MODEL, n_chunks, GATE_CHUNK).transpose(1, 0, 2)
    bg = bg.reshape(n_chunks, 1, GATE_CHUNK)
    cw = cw.reshape(CONV_K, n_lb, LANES).transpose(1, 0, 2)
    cb = cb.reshape(n_lb, 1, LANES)
    p_in = _conv_layout_selector(tm)
    full = lambda arr: pl.BlockSpec(arr.shape, lambda i: (0,) * arr.ndim)
    return pl.pallas_call(
        functools.partial(_mixer_kernel, tiles_per_seq=tiles_per_seq),
        grid=(m // tm,),
        in_specs=[
            pl.BlockSpec((tm, D_MODEL), lambda i: (i, 0)),
            pl.BlockSpec((tm, D_ATTN), lambda i: (i, 0)),
            pl.BlockSpec((tm, CONV_CH), lambda i: (i, 0)),
            pl.BlockSpec((CONV_HALO, CONV_CH), lambda i: (jnp.maximum(i * hb - 1, 0), 0)),
            pl.BlockSpec((CONV_HALO, CONV_CH), lambda i: (jnp.minimum((i + 1) * hb, n_hb - 1), 0)),
            full(g1), full(wg), full(bg), full(cw), full(cb), full(lg), full(lb),
            full(wna), full(wco), full(wo), full(p_in),
        ],
        out_specs=pl.BlockSpec((tm, D_MODEL), lambda i: (i, 0)),
        out_shape=jax.ShapeDtypeStruct((m, D_MODEL), F32),
        scratch_shapes=[
            pltpu.VMEM((n_lb, tm + 2 * (CONV_K // 2) * SUBLANES, LANES), F32),
            pltpu.VMEM((n_lb, tm, LANES), F32),
            pltpu.VMEM((tm, CONV_CH), BF16),
            pltpu.VMEM((n_chunks, tm, GATE_CHUNK), F32),
            pltpu.VMEM((tm, D_MODEL), BF16),
        ],
        compiler_params=pltpu.CompilerParams(
            dimension_semantics=("arbitrary",), vmem_limit_bytes=VMEM_LIMIT_BYTES),
        name="mixer",
    )(x2d, a2d, u2d, u2d, u2d, g1, wg, bg, cw, cb, lg, lb, wna, wco, wo, p_in)


def _ffn_kernel(x_ref, xp_ref, xn_ref, g2_ref, wup_ref, cw_ref, cb_ref, wdn_ref, gf_ref, o_ref,
                act_ref, xs_ref, *, tiles_per_seq, final_norm):
    assert FFN_CONV_K == 3 and FFN_HALO == SUBLANES
    tm = x_ref.shape[0]
    n_lb = D_MODEL // LANES
    group = SUBLANES * SUBLANES
    n_groups = tm // group
    t = pl.program_id(0) % tiles_per_seq
    not_first = (t != 0).astype(F32)
    not_last = (t != tiles_per_seq - 1).astype(F32)
    lanes = lambda lb: slice(lb * LANES, (lb + 1) * LANES)
    sub = lax.broadcasted_iota(jnp.int32, (SUBLANES, FFN_CHUNK), 0)

    def regroup(rows):
        for lb in range(n_lb):
            xs_ref[lb] = rows[:, lanes(lb)]
        return jnp.concatenate([
            jnp.concatenate([xs_ref[lb, pl.ds(k * SUBLANES - (k % SUBLANES) * (SUBLANES - 1), SUBLANES,
                                              stride=SUBLANES), :]
                             for lb in range(n_lb)], axis=1)
            for k in range(tm // SUBLANES)], axis=0)

    x = regroup(x_ref[...])
    g2 = g2_ref[...]
    h_ext = jnp.concatenate([
        (_rms_scale(x) * g2).astype(BF16),
        jnp.concatenate([_rms_scale(xp_ref[...]) * g2 * not_first,
                         _rms_scale(xn_ref[...]) * g2 * not_last], axis=0).astype(BF16),
    ], axis=0)

    def conv_up(col0):
        cols = slice(col0, col0 + FFN_CHUNK)
        up = jnp.dot(h_ext, wup_ref[:, cols], preferred_element_type=F32)
        vreg = lambda k: up[k * SUBLANES:(k + 1) * SUBLANES]
        first = lambda g: vreg(g * SUBLANES)
        last = lambda g: vreg(g * SUBLANES + SUBLANES - 1)
        halo_prev, halo_next = vreg(tm // SUBLANES), vreg(tm // SUBLANES + 1)
        down = [pltpu.roll(halo_prev, 1, axis=0)] + [pltpu.roll(last(g), 1, axis=0) for g in range(n_groups)]
        up_ = [pltpu.roll(first(g), SUBLANES - 1, axis=0) for g in range(n_groups)] + [
            pltpu.roll(halo_next, SUBLANES - 1, axis=0)]
        prev, nxt = [], []
        for g in range(n_groups):
            lo, hi = g * group, (g + 1) * group
            prev += [jnp.where(sub == 0, down[g], down[g + 1]), up[lo:hi - SUBLANES]]
            nxt += [up[lo + SUBLANES:hi], jnp.where(sub == SUBLANES - 1, up_[g + 1], up_[g])]
        return (cb_ref[:, cols] + cw_ref[0:1, cols] * jnp.concatenate(prev, axis=0)
                + cw_ref[1:2, cols] * up[:tm] + cw_ref[2:3, cols] * jnp.concatenate(nxt, axis=0))

    for c in range(D_FF // FFN_CHUNK):
        gate = conv_up(c * FFN_CHUNK)
        val = conv_up(D_FF + c * FFN_CHUNK)
        act_ref[:, c * FFN_CHUNK:(c + 1) * FFN_CHUNK] = (_twice_gelu_exact(gate) * val).astype(BF16)

    y = x + jnp.dot(act_ref[...], wdn_ref[...], preferred_element_type=F32)
    if final_norm:
        y = _rms_scale(y) * gf_ref[...]
    o_ref[...] = regroup(y)


def _ffn(x2d, seq, g2, wup, cw, cb, wdn, gf, final_norm):
    m = x2d.shape[0]
    tm = TM_FFN
    assert seq % tm == 0
    tiles_per_seq = seq // tm
    hb = tm // FFN_HALO
    n_hb = m // FFN_HALO
    const = lambda i: (0, 0)
    full = lambda arr: pl.BlockSpec(arr.shape, const)
    resident = lambda arr: pl.BlockSpec(arr.shape, const, pipeline_mode=pl.Buffered(1))
    return pl.pallas_call(
        functools.partial(_ffn_kernel, tiles_per_seq=tiles_per_seq, final_norm=final_norm),
        grid=(m // tm,),
        in_specs=[
            pl.BlockSpec((tm, D_MODEL), lambda i: (i, 0)),
            pl.BlockSpec((FFN_HALO, D_MODEL), lambda i: (jnp.maximum(i * hb - 1, 0), 0)),
            pl.BlockSpec((FFN_HALO, D_MODEL), lambda i: (jnp.minimum((i + 1) * hb, n_hb - 1), 0)),
            full(g2), resident(wup), full(cw), full(cb), resident(wdn), full(gf),
        ],
        out_specs=pl.BlockSpec((tm, D_MODEL), lambda i: (i, 0)),
        out_shape=jax.ShapeDtypeStruct((m, D_MODEL), F32),
        scratch_shapes=[
            pltpu.VMEM((tm, D_FF), BF16),
            pltpu.VMEM((D_MODEL // LANES, tm, LANES), F32),
        ],
        compiler_params=pltpu.CompilerParams(
            dimension_semantics=("arbitrary",), vmem_limit_bytes=VMEM_LIMIT_BYTES),
        name="ffn",
    )(x2d, x2d, x2d, g2, wup, cw, cb, wdn, gf)


def kernel(x, norm1_g, w_in, b_in, rpb, w_na_out, conv_dw_w, conv_dw_b, conv_ln_g, conv_ln_b,
           w_conv_out, w_out, norm2_g, w_up, ffn_dw_w, ffn_dw_b, w_down, norm_f_g):
    bsz, seq, d = x.shape
    depth = w_in.shape[0]
    n_tok = 3 * D_ATTN + 2 * CONV_CH
    row = lambda v: v.reshape(1, -1).astype(F32)
    x2d = x.reshape(bsz * seq, d)
    for l in range(depth):
        w_in_l = w_in[l].astype(BF16)
        b_in_l = row(b_in[l])
        g1 = row(norm1_g[l])
        q, k, v, u = _inproj(x2d, g1, w_in_l[:, :n_tok], b_in_l[:, :n_tok])
        shp = (bsz, seq, D_ATTN)
        a = _natten(q.reshape(shp), k.reshape(shp), v.reshape(shp), _natten_bias_table(rpb[l]))
        x2d = _mixer(
            x2d, a.reshape(bsz * seq, D_ATTN), u, seq, g1, w_in_l[:, n_tok:], b_in_l[:, n_tok:],
            conv_dw_w[l].astype(F32), row(conv_dw_b[l]), row(conv_ln_g[l]), row(conv_ln_b[l]),
            w_na_out[l].astype(BF16), w_conv_out[l].astype(BF16), w_out[l].astype(BF16))
        x2d = _ffn(
            x2d, seq, row(norm2_g[l]), w_up[l].astype(BF16), ffn_dw_w[l].astype(F32),
            row(ffn_dw_b[l]), (0.5 * w_down[l]).astype(BF16), row(norm_f_g),
            final_norm=(l == depth - 1))
    return x2d.reshape(bsz, seq, d)
```

```python
import functools

import jax
import jax.numpy as jnp
import numpy as np
from jax import lax
from jax.experimental import pallas as pl
from jax.experimental.pallas import tpu as pltpu

F32 = jnp.float32
BF16 = jnp.bfloat16

D_MODEL = 1024
GRID_W = 64
NA_HEADS = 8
NA_HEAD_DIM = 64
D_ATTN = NA_HEADS * NA_HEAD_DIM
NA_KH = 8
NA_KW = 16
CONV_CH = 512
CONV_K = 31
D_FF = 2816
FFN_CONV_K = 3
EPS = 1e-6
NEG = -1e30
LOG2_E = 1.4426950408889634

LANES = 128
SUBLANES = 8
BF16_SUBLANES = 16
VMEM_LIMIT_BYTES = 56 * 1024 * 1024

TM_INPROJ = 1024
TM_MIXER = 512
TM_FFN = 512
FFN_SPLIT = 1
NATTEN_ROWS_PER_STEP = 8
NATTEN_ROWS_PER_ITER = 4
CONV_VREG_BLOCK = 8
GATE_CHUNK = 1024
CONV_HALO = 16
FFN_HALO = 8
FFN_CHUNK = 256


def _rms_scale(x):
    return x * lax.rsqrt(jnp.mean(x * x, axis=-1, keepdims=True) + EPS)


def _twice_gelu_exact(x):
    return x * (1.0 + lax.erf(x * np.float32(np.sqrt(0.5))))


def _inproj_kernel(x_ref, g_ref, w_ref, b_ref, q_ref, k_ref, v_ref, u_ref):
    h = (_rms_scale(x_ref[...]) * g_ref[...]).astype(BF16)

    def proj(n):
        cols = slice(n * D_ATTN, (n + 1) * D_ATTN)
        return jnp.dot(h, w_ref[:, cols], preferred_element_type=F32) + b_ref[:, cols]

    q_ref[...] = (proj(0) * (NA_HEAD_DIM ** -0.5 * LOG2_E)).astype(BF16)
    k_ref[...] = proj(1).astype(BF16)
    v_ref[...] = proj(2).astype(BF16)
    u_ref[...] = (proj(3) * jax.nn.sigmoid(proj(4))).astype(BF16)


def _inproj(x2d, g, w, b):
    m = x2d.shape[0]
    tm = TM_INPROJ
    n_in = w.shape[1]
    const = lambda i: (0, 0)
    out = jax.ShapeDtypeStruct((m, D_ATTN), BF16)
    row_spec = pl.BlockSpec((tm, D_ATTN), lambda i: (i, 0))
    return pl.pallas_call(
        _inproj_kernel,
        grid=(m // tm,),
        in_specs=[
            pl.BlockSpec((tm, D_MODEL), lambda i: (i, 0)),
            pl.BlockSpec((1, D_MODEL), const),
            pl.BlockSpec((D_MODEL, n_in), const),
            pl.BlockSpec((1, n_in), const),
        ],
        out_specs=[row_spec, row_spec, row_spec, row_spec],
        out_shape=[out, out, out, out],
        compiler_params=pltpu.CompilerParams(
            dimension_semantics=("arbitrary",), vmem_limit_bytes=VMEM_LIMIT_BYTES),
        name="inproj",
    )(x2d, g, w, b)


def _natten_bias_table(rpb):
    col = jnp.arange(GRID_W)
    col_start = jnp.clip(col - NA_KW // 2, 0, GRID_W - NA_KW)
    kc = col[None, :]
    in_win = (kc >= col_start[:, None]) & (kc < col_start[:, None] + NA_KW)
    reach = GRID_W - NA_KW
    padded = jnp.pad(rpb.astype(F32) * LOG2_E, ((0, 0), (0, 0), (reach, reach)))
    t = jnp.stack([padded[:, :, GRID_W - 1 - c:2 * GRID_W - 1 - c] for c in range(GRID_W)], axis=2)
    t = jnp.where(in_win[None, None], t, NEG)
    nxt = jnp.concatenate([t[:, 1:], jnp.full_like(t[:, :1], NEG)], axis=1)
    return jnp.concatenate([t, nxt], axis=-1)


def _natten_kernel(q_ref, k_ref, v_ref, tb_ref, o_ref, *, rows):
    step = pl.program_id(1)
    lane = lax.broadcasted_iota(jnp.int32, (GRID_W, LANES), 1)
    lo_half = lane < NA_HEAD_DIM
    n_keys = NA_KH * GRID_W

    def row_group(it, carry):
        jobs = []
        for sub in range(NATTEN_ROWS_PER_ITER):
            rr = it * NATTEN_ROWS_PER_ITER + sub
            r = step * NATTEN_ROWS_PER_STEP + rr
            rs = jnp.clip(r - NA_KH // 2, 0, rows - NA_KH)
            base = rs - r + NA_KH - 1
            q0 = pl.multiple_of(rr * GRID_W, GRID_W)
            k0 = pl.multiple_of(rs * GRID_W, GRID_W)
            jobs += [(q0, k0, base, p) for p in range(NA_HEADS // 2)]

        scores = []
        for q0, k0, base, p in jobs:
            cols = slice(p * LANES, (p + 1) * LANES)
            qp = q_ref[0, pl.ds(q0, GRID_W), cols]
            kp = k_ref[0, pl.ds(k0, n_keys), cols]
            zero = jnp.zeros_like(qp)
            q2 = jnp.concatenate([jnp.where(lo_half, qp, zero), jnp.where(lo_half, zero, qp)], axis=0)
            s = lax.dot_general(q2, kp, (((1,), (1,)), ((), ())), preferred_element_type=F32)
            bias = jnp.concatenate(
                [jnp.concatenate([tb_ref[2 * p + hh, base + 2 * j] for j in range(NA_KH // 2)], axis=1)
                 for hh in range(2)], axis=0)
            scores.append(s + bias)

        probs = []
        for s in scores:
            e = jnp.exp2(s - jnp.max(s, axis=-1, keepdims=True))
            probs.append((e.astype(BF16), jnp.sum(e, axis=-1, keepdims=True)))

        outs = []
        for (e, l), (q0, k0, base, p) in zip(probs, jobs):
            vp = v_ref[0, pl.ds(k0, n_keys), p * LANES:(p + 1) * LANES]
            o2 = jnp.dot(e, vp, preferred_element_type=F32) / l
            outs.append(jnp.where(lo_half, o2[:GRID_W], o2[GRID_W:]))

        pairs = NA_HEADS // 2
        for sub in range(NATTEN_ROWS_PER_ITER):
            q0 = jobs[sub * pairs][0]
            o_ref[0, pl.ds(q0, GRID_W), :] = jnp.concatenate(
                outs[sub * pairs:(sub + 1) * pairs], axis=1).astype(o_ref.dtype)
        return carry

    lax.fori_loop(0, NATTEN_ROWS_PER_STEP // NATTEN_ROWS_PER_ITER, row_group, 0)


def _natten(q, k, v, table):
    bsz, s, _ = q.shape
    rows = s // GRID_W
    assert rows >= NA_KH and rows % NATTEN_ROWS_PER_STEP == 0
    tq = NATTEN_ROWS_PER_STEP * GRID_W
    return pl.pallas_call(
        functools.partial(_natten_kernel, rows=rows),
        grid=(bsz, rows // NATTEN_ROWS_PER_STEP),
        in_specs=[
            pl.BlockSpec((1, tq, D_ATTN), lambda b, i: (b, i, 0)),
            pl.BlockSpec((1, s, D_ATTN), lambda b, i: (b, 0, 0)),
            pl.BlockSpec((1, s, D_ATTN), lambda b, i: (b, 0, 0)),
            pl.BlockSpec(table.shape, lambda b, i: (0, 0, 0, 0)),
        ],
        out_specs=pl.BlockSpec((1, tq, D_ATTN), lambda b, i: (b, i, 0)),
        out_shape=jax.ShapeDtypeStruct((bsz, s, D_ATTN), BF16),
        compiler_params=pltpu.CompilerParams(
            dimension_semantics=("arbitrary", "arbitrary"), vmem_limit_bytes=VMEM_LIMIT_BYTES),
        name="natten",
    )(q, k, v, table)


def _mixer_kernel(x_ref, a_ref, u_ref, up_ref, un_ref, g1_ref, wg_ref, bg_ref, cw_ref, cb_ref,
                  lg_ref, lb_ref, wna_ref, wco_ref, wo_ref, pin_ref, o_ref, d_ref, y_ref, u2_ref,
                  gates_ref, h_ref, *, tiles_per_seq):
    tm = x_ref.shape[0]
    run = tm // SUBLANES
    pad = CONV_K // 2
    n_lb = CONV_CH // LANES
    t = pl.program_id(0) % tiles_per_seq
    prev = up_ref[...].astype(F32) * (t != 0).astype(F32)
    nxt = un_ref[...].astype(F32) * (t != tiles_per_seq - 1).astype(F32)

    d_main = jnp.dot(pin_ref[...], u_ref[...], preferred_element_type=F32)
    sub = lax.broadcasted_iota(jnp.int32, (SUBLANES, LANES), 0)
    vreg = lambda v, cols: d_main[v * SUBLANES:(v + 1) * SUBLANES, cols]
    for lb in range(n_lb):
        cols = slice(lb * LANES, (lb + 1) * LANES)
        d_ref[lb, pad * SUBLANES:pad * SUBLANES + tm, :] = d_main[:, cols]
        for j in range(pad):
            lo = jnp.where(sub == 0, prev[CONV_HALO - pad + j:CONV_HALO - pad + j + 1, cols],
                           pltpu.roll(vreg(run - pad + j, cols), 1, axis=0))
            d_ref[lb, j * SUBLANES:(j + 1) * SUBLANES, :] = lo
            hi = jnp.where(sub == SUBLANES - 1, nxt[j:j + 1, cols],
                           pltpu.roll(vreg(j, cols), SUBLANES - 1, axis=0))
            d_ref[lb, (pad + run + j) * SUBLANES:(pad + run + j + 1) * SUBLANES, :] = hi

    x = x_ref[...]
    h_ref[...] = (_rms_scale(x) * g1_ref[...]).astype(BF16)

    n_acc = CONV_VREG_BLOCK
    blocks_per_lb = run // n_acc
    n_chunks = gates_ref.shape[0]
    blocks_per_chunk = n_lb * blocks_per_lb // n_chunks

    def conv_block(blk):
        lb = blk // blocks_per_lb
        row0 = pl.multiple_of((blk % blocks_per_lb) * (n_acc * SUBLANES), n_acc * SUBLANES)
        w = [cw_ref[lb, kk:kk + 1, :] for kk in range(CONV_K)]
        accs = [cb_ref[lb] + jnp.zeros((SUBLANES, LANES), F32)] * n_acc
        for i in range(n_acc + CONV_K - 1):
            dv = d_ref[lb, pl.ds(row0 + i * SUBLANES, SUBLANES), :]
            for j in range(n_acc):
                if 0 <= i - j < CONV_K:
                    accs[j] = accs[j] + w[i - j] * dv
        return lb, row0, accs

    def chunk(c, carry):
        convs = [conv_block(c * blocks_per_chunk + b) for b in range(blocks_per_chunk)]
        g = jnp.dot(h_ref[...], wg_ref[c], preferred_element_type=F32) + bg_ref[c]
        for lb, row0, accs in convs:
            for j in range(n_acc):
                y_ref[lb, pl.ds(row0 + j * SUBLANES, SUBLANES), :] = accs[j]
        gates_ref[c] = g
        return carry

    lax.fori_loop(0, n_chunks, chunk, 0)

    half = n_chunks // 2
    gate = lambda first: jax.nn.sigmoid(
        jnp.concatenate([gates_ref[first + n] for n in range(half)], axis=1))
    gated_a = gate(0) * jnp.dot(a_ref[...], wna_ref[...], preferred_element_type=F32)

    y = jnp.concatenate([y_ref[lb] for lb in range(n_lb)], axis=1)
    mu = jnp.mean(y, axis=-1, keepdims=True)
    dlt = y - mu
    var = jnp.mean(dlt * dlt, axis=-1, keepdims=True)
    z = dlt * lax.rsqrt(var + EPS) * lg_ref[...] + lb_ref[...]
    z = z * jax.nn.sigmoid(z)
    for lb in range(n_lb):
        y_ref[lb] = z[:, lb * LANES:(lb + 1) * LANES]
    for lb in range(n_lb):
        for s in range(SUBLANES):
            for v0 in range(0, run, BF16_SUBLANES):
                u2_ref[s * run + v0:s * run + v0 + BF16_SUBLANES, lb * LANES:(lb + 1) * LANES] = (
                    y_ref[lb, pl.ds(v0 * SUBLANES + s, BF16_SUBLANES, stride=SUBLANES), :].astype(BF16))

    br_b = jnp.dot(u2_ref[...], wco_ref[...], preferred_element_type=F32)
    merged = gated_a + gate(half) * br_b
    o_ref[...] = x + jnp.dot(merged.astype(BF16), wo_ref[...], preferred_element_type=F32)


def _conv_layout_selector(tm):
    run = tm // SUBLANES
    v, s = np.meshgrid(np.arange(run), np.arange(SUBLANES), indexing="ij")
    p_in = np.zeros((tm, tm), np.float32)
    p_in[(v * SUBLANES + s).ravel(), (s * run + v).ravel()] = 1.0
    return jnp.asarray(p_in, BF16)


def _mixer(x2d, a2d, u2d, seq, g1, wg, bg, cw, cb, lg, lb, wna, wco, wo):
    m = x2d.shape[0]
    tm = TM_MIXER
    run = tm // SUBLANES
    assert seq % tm == 0 and run % BF16_SUBLANES == 0 and run % CONV_VREG_BLOCK == 0
    tiles_per_seq = seq // tm
    hb = tm // CONV_HALO
    n_hb = m // CONV_HALO
    n_lb = CONV_CH // LANES
    n_chunks = 2 * D_MODEL // GATE_CHUNK
    assert (n_lb * (run // CONV_VREG_BLOCK)) % n_chunks == 0
    wg = wg.reshape(D_MODEL, n_chunks, GATE_CHUNK).transpose(1, 0, 2)
    bg = bg.reshape(n_chunks, 1, GATE_CHUNK)
    cw = cw.reshape(CONV_K, n_lb, LANES).transpose(1, 0, 2)
    cb = cb.reshape(n_lb, 1, LANES)
    p_in = _conv_layout_selector(tm)
    full = lambda arr: pl.BlockSpec(arr.shape, lambda i: (0,) * arr.ndim)
    return pl.pallas_call(
        functools.partial(_mixer_kernel, tiles_per_seq=tiles_per_seq),
        grid=(m // tm,),
        in_specs=[
            pl.BlockSpec((tm, D_MODEL), lambda i: (i, 0)),
            pl.BlockSpec((tm, D_ATTN), lambda i: (i, 0)),
            pl.BlockSpec((tm, CONV_CH), lambda i: (i, 0)),
            pl.BlockSpec((CONV_HALO, CONV_CH), lambda i: (jnp.maximum(i * hb - 1, 0), 0)),
            pl.BlockSpec((CONV_HALO, CONV_CH), lambda i: (jnp.minimum((i + 1) * hb, n_hb - 1), 0)),
            full(g1), full(wg), full(bg), full(cw), full(cb), full(lg), full(lb),
            full(wna), full(wco), full(wo), full(p_in),
        ],
        out_specs=pl.BlockSpec((tm, D_MODEL), lambda i: (i, 0)),
        out_shape=jax.ShapeDtypeStruct((m, D_MODEL), F32),
        scratch_shapes=[
            pltpu.VMEM((n_lb, tm + 2 * (CONV_K // 2) * SUBLANES, LANES), F32),
            pltpu.VMEM((n_lb, tm, LANES), F32),
            pltpu.VMEM((tm, CONV_CH), BF16),
            pltpu.VMEM((n_chunks, tm, GATE_CHUNK), F32),
            pltpu.VMEM((tm, D_MODEL), BF16),
        ],
        compiler_params=pltpu.CompilerParams(
            dimension_semantics=("arbitrary",), vmem_limit_bytes=VMEM_LIMIT_BYTES),
        name="mixer",
    )(x2d, a2d, u2d, u2d, u2d, g1, wg, bg, cw, cb, lg, lb, wna, wco, wo, p_in)


def _ffn_kernel(x_ref, xp_ref, xn_ref, g2_ref, wup_ref, cw_ref, cb_ref, wdn_ref, gf_ref, o_ref,
                act_ref, xs_ref, *, tiles_per_seq, final_norm):
    assert FFN_CONV_K == 3 and FFN_HALO == SUBLANES
    ts = x_ref.shape[0] // FFN_SPLIT
    n_lb = D_MODEL // LANES
    group = SUBLANES * SUBLANES
    n_groups = ts // group
    t = pl.program_id(0) % tiles_per_seq
    lanes = lambda lb: slice(lb * LANES, (lb + 1) * LANES)
    sub = lax.broadcasted_iota(jnp.int32, (SUBLANES, FFN_CHUNK), 0)
    g2 = g2_ref[...]

    def regroup(rows, slot):
        for lb in range(n_lb):
            xs_ref[slot, lb] = rows[:, lanes(lb)]
        return jnp.concatenate([
            jnp.concatenate([xs_ref[slot, lb, pl.ds(k * SUBLANES - (k % SUBLANES) * (SUBLANES - 1),
                                                    SUBLANES, stride=SUBLANES), :]
                             for lb in range(n_lb)], axis=1)
            for k in range(ts // SUBLANES)], axis=0)

    norm = lambda rows: _rms_scale(rows) * g2

    def prologue(slot):
        r0 = slot * ts
        before = (norm(xp_ref[...]) * (t != 0).astype(F32) if slot == 0
                  else norm(x_ref[r0 - FFN_HALO:r0, :]))
        after = (norm(xn_ref[...]) * (t != tiles_per_seq - 1).astype(F32) if slot == FFN_SPLIT - 1
                 else norm(x_ref[r0 + ts:r0 + ts + FFN_HALO, :]))
        x = regroup(x_ref[r0:r0 + ts, :], slot)
        h_ext = jnp.concatenate([
            norm(x).astype(BF16), jnp.concatenate([before, after], axis=0).astype(BF16)], axis=0)
        return x, h_ext

    def up_chunk(slot, h_ext, c):
        def conv_up(col0):
            cols = slice(col0, col0 + FFN_CHUNK)
            up = jnp.dot(h_ext, wup_ref[:, cols], preferred_element_type=F32)
            vreg = lambda k: up[k * SUBLANES:(k + 1) * SUBLANES]
            first = lambda g: vreg(g * SUBLANES)
            last = lambda g: vreg(g * SUBLANES + SUBLANES - 1)
            halo_prev, halo_next = vreg(ts // SUBLANES), vreg(ts // SUBLANES + 1)
            down = [pltpu.roll(halo_prev, 1, axis=0)] + [
                pltpu.roll(last(g), 1, axis=0) for g in range(n_groups)]
            up_ = [pltpu.roll(first(g), SUBLANES - 1, axis=0) for g in range(n_groups)] + [
                pltpu.roll(halo_next, SUBLANES - 1, axis=0)]
            prev, nxt = [], []
            for g in range(n_groups):
                lo, hi = g * group, (g + 1) * group
                prev += [jnp.where(sub == 0, down[g], down[g + 1]), up[lo:hi - SUBLANES]]
                nxt += [up[lo + SUBLANES:hi], jnp.where(sub == SUBLANES - 1, up_[g + 1], up_[g])]
            return (cb_ref[:, cols] + cw_ref[0:1, cols] * jnp.concatenate(prev, axis=0)
                    + cw_ref[1:2, cols] * up[:ts] + cw_ref[2:3, cols] * jnp.concatenate(nxt, axis=0))

        gate = conv_up(c * FFN_CHUNK)
        val = conv_up(D_FF + c * FFN_CHUNK)
        act_ref[slot, :, c * FFN_CHUNK:(c + 1) * FFN_CHUNK] = (
            _twice_gelu_exact(gate) * val).astype(BF16)

    def down(slot, x):
        y = x + jnp.dot(act_ref[slot], wdn_ref[...], preferred_element_type=F32)
        return _rms_scale(y) * gf_ref[...] if final_norm else y

    def epilogue(slot, y):
        o_ref[slot * ts:(slot + 1) * ts, :] = regroup(y, slot)

    for slot in range(FFN_SPLIT):
        x, h_ext = prologue(slot)
        for c in range(D_FF // FFN_CHUNK):
            up_chunk(slot, h_ext, c)
        epilogue(slot, down(slot, x))


def _ffn(x2d, seq, g2, wup, cw, cb, wdn, gf, final_norm):
    m = x2d.shape[0]
    tm = TM_FFN
    assert seq % tm == 0
    tiles_per_seq = seq // tm
    hb = tm // FFN_HALO
    n_hb = m // FFN_HALO
    const = lambda i: (0, 0)
    full = lambda arr: pl.BlockSpec(arr.shape, const)
    resident = lambda arr: pl.BlockSpec(arr.shape, const, pipeline_mode=pl.Buffered(1))
    return pl.pallas_call(
        functools.partial(_ffn_kernel, tiles_per_seq=tiles_per_seq, final_norm=final_norm),
        grid=(m // tm,),
        in_specs=[
            pl.BlockSpec((tm, D_MODEL), lambda i: (i, 0)),
            pl.BlockSpec((FFN_HALO, D_MODEL), lambda i: (jnp.maximum(i * hb - 1, 0), 0)),
            pl.BlockSpec((FFN_HALO, D_MODEL), lambda i: (jnp.minimum((i + 1) * hb, n_hb - 1), 0)),
            full(g2), resident(wup), full(cw), full(cb), resident(wdn), full(gf),
        ],
        out_specs=pl.BlockSpec((tm, D_MODEL), lambda i: (i, 0)),
        out_shape=jax.ShapeDtypeStruct((m, D_MODEL), F32),
        scratch_shapes=[
            pltpu.VMEM((FFN_SPLIT, tm // FFN_SPLIT, D_FF), BF16),
            pltpu.VMEM((FFN_SPLIT, D_MODEL // LANES, tm // FFN_SPLIT, LANES), F32),
        ],
        compiler_params=pltpu.CompilerParams(
            dimension_semantics=("arbitrary",), vmem_limit_bytes=VMEM_LIMIT_BYTES),
        name="ffn",
    )(x2d, x2d, x2d, g2, wup, cw, cb, wdn, gf)


def kernel(x, norm1_g, w_in, b_in, rpb, w_na_out, conv_dw_w, conv_dw_b, conv_ln_g, conv_ln_b,
           w_conv_out, w_out, norm2_g, w_up, ffn_dw_w, ffn_dw_b, w_down, norm_f_g):
    bsz, seq, d = x.shape
    depth = w_in.shape[0]
    n_tok = 3 * D_ATTN + 2 * CONV_CH
    row = lambda v: v.reshape(1, -1).astype(F32)
    x2d = x.reshape(bsz * seq, d)
    for l in range(depth):
        w_in_l = w_in[l].astype(BF16)
        b_in_l = row(b_in[l])
        g1 = row(norm1_g[l])
        q, k, v, u = _inproj(x2d, g1, w_in_l[:, :n_tok], b_in_l[:, :n_tok])
        shp = (bsz, seq, D_ATTN)
        a = _natten(q.reshape(shp), k.reshape(shp), v.reshape(shp), _natten_bias_table(rpb[l]))
        x2d = _mixer(
            x2d, a.reshape(bsz * seq, D_ATTN), u, seq, g1, w_in_l[:, n_tok:], b_in_l[:, n_tok:],
            conv_dw_w[l].astype(F32), row(conv_dw_b[l]), row(conv_ln_g[l]), row(conv_ln_b[l]),
            w_na_out[l].astype(BF16), w_conv_out[l].astype(BF16), w_out[l].astype(BF16))
        x2d = _ffn(
            x2d, seq, row(norm2_g[l]), w_up[l].astype(BF16), ffn_dw_w[l].astype(F32),
            row(ffn_dw_b[l]), (0.5 * w_down[l]).astype(BF16), row(norm_f_g),
            final_norm=(l == depth - 1))
    return x2d.reshape(bsz, seq, d)
```

```python
import functools

import jax
import jax.numpy as jnp
import numpy as np
from jax import lax
from jax.experimental import pallas as pl
from jax.experimental.pallas import tpu as pltpu

F32 = jnp.float32
BF16 = jnp.bfloat16

D_MODEL = 1024
GRID_W = 64
NA_HEADS = 8
NA_HEAD_DIM = 64
D_ATTN = NA_HEADS * NA_HEAD_DIM
NA_KH = 8
NA_KW = 16
CONV_CH = 512
CONV_K = 31
D_FF = 2816
FFN_CONV_K = 3
EPS = 1e-6
NEG = -1e30
LOG2_E = 1.4426950408889634

LANES = 128
SUBLANES = 8
BF16_SUBLANES = 16
VMEM_LIMIT_BYTES = 56 * 1024 * 1024

TM_INPROJ = 1024
TM_MIXER = 512
TM_FFN = 512
FFN_SPLIT = 1
NATTEN_ROWS_PER_STEP = 16
NATTEN_ROWS_PER_ITER = 16
NATTEN_PIPE_DEPTH = 2
CONV_VREG_BLOCK = 8
GATE_CHUNK = 1024
CONV_HALO = 16
FFN_HALO = 8
FFN_CHUNK = 256


def _rms_inv(x):
    return lax.rsqrt(jnp.mean(x * x, axis=-1, keepdims=True) + EPS)


def _rms_scale(x):
    return x * _rms_inv(x)


def _twice_gelu_exact(x):
    return x * (1.0 + lax.erf(x * np.float32(np.sqrt(0.5))))


def _inproj_kernel(x_ref, g_ref, w_ref, b_ref, q_ref, k_ref, v_ref, u_ref):
    h = (_rms_scale(x_ref[...]) * g_ref[...]).astype(BF16)

    def proj(n):
        cols = slice(n * D_ATTN, (n + 1) * D_ATTN)
        return jnp.dot(h, w_ref[:, cols], preferred_element_type=F32) + b_ref[:, cols]

    q_ref[...] = (proj(0) * (NA_HEAD_DIM ** -0.5 * LOG2_E)).astype(BF16)
    k_ref[...] = proj(1).astype(BF16)
    v_ref[...] = proj(2).astype(BF16)
    u_ref[...] = (proj(3) * jax.nn.sigmoid(proj(4))).astype(BF16)


def _inproj(x2d, g, w, b):
    m = x2d.shape[0]
    tm = TM_INPROJ
    n_in = w.shape[1]
    const = lambda i: (0, 0)
    out = jax.ShapeDtypeStruct((m, D_ATTN), BF16)
    row_spec = pl.BlockSpec((tm, D_ATTN), lambda i: (i, 0))
    return pl.pallas_call(
        _inproj_kernel,
        grid=(m // tm,),
        in_specs=[
            pl.BlockSpec((tm, D_MODEL), lambda i: (i, 0)),
            pl.BlockSpec((1, D_MODEL), const),
            pl.BlockSpec((D_MODEL, n_in), const),
            pl.BlockSpec((1, n_in), const),
        ],
        out_specs=[row_spec, row_spec, row_spec, row_spec],
        out_shape=[out, out, out, out],
        compiler_params=pltpu.CompilerParams(
            dimension_semantics=("arbitrary",), vmem_limit_bytes=VMEM_LIMIT_BYTES),
        name="inproj",
    )(x2d, g, w, b)


def _natten_bias_table(rpb):
    col = jnp.arange(GRID_W)
    col_start = jnp.clip(col - NA_KW // 2, 0, GRID_W - NA_KW)
    kc = col[None, :]
    in_win = (kc >= col_start[:, None]) & (kc < col_start[:, None] + NA_KW)
    reach = GRID_W - NA_KW
    padded = jnp.pad(rpb.astype(F32) * LOG2_E, ((0, 0), (0, 0), (reach, reach)))
    t = jnp.stack([padded[:, :, GRID_W - 1 - c:2 * GRID_W - 1 - c] for c in range(GRID_W)], axis=2)
    t = jnp.where(in_win[None, None], t, NEG)
    nxt = jnp.concatenate([t[:, 1:], jnp.full_like(t[:, :1], NEG)], axis=1)
    return jnp.concatenate([t, nxt], axis=-1)


def _natten_kernel(q_ref, k_ref, v_ref, tb_ref, o_ref, *, rows):
    step = pl.program_id(1)
    lane = lax.broadcasted_iota(jnp.int32, (GRID_W, LANES), 1)
    lo_half = lane < NA_HEAD_DIM
    n_keys = NA_KH * GRID_W

    def row_group(it, carry):
        jobs = []
        for sub in range(NATTEN_ROWS_PER_ITER):
            rr = it * NATTEN_ROWS_PER_ITER + sub
            r = step * NATTEN_ROWS_PER_STEP + rr
            rs = jnp.clip(r - NA_KH // 2, 0, rows - NA_KH)
            base = rs - r + NA_KH - 1
            q0 = pl.multiple_of(rr * GRID_W, GRID_W)
            k0 = pl.multiple_of(rs * GRID_W, GRID_W)
            jobs += [(q0, k0, base, p) for p in range(NA_HEADS // 2)]

        def score(job):
            q0, k0, base, p = job
            cols = slice(p * LANES, (p + 1) * LANES)
            qp = q_ref[0, pl.ds(q0, GRID_W), cols]
            kp = k_ref[0, pl.ds(k0, n_keys), cols]
            zero = jnp.zeros_like(qp)
            q2 = jnp.concatenate([jnp.where(lo_half, qp, zero), jnp.where(lo_half, zero, qp)], axis=0)
            s = lax.dot_general(q2, kp, (((1,), (1,)), ((), ())), preferred_element_type=F32)
            bias = jnp.concatenate(
                [jnp.concatenate([tb_ref[2 * p + hh, base + 2 * j] for j in range(NA_KH // 2)], axis=1)
                 for hh in range(2)], axis=0)
            return s + bias

        def softmax(s):
            e = jnp.exp2(s - jnp.max(s, axis=-1, keepdims=True))
            return e.astype(BF16), jnp.sum(e, axis=-1, keepdims=True)

        def weighted_values(prob, job):
            e, l = prob
            q0, k0, base, p = job
            vp = v_ref[0, pl.ds(k0, n_keys), p * LANES:(p + 1) * LANES]
            o2 = jnp.dot(e, vp, preferred_element_type=F32) / l
            return jnp.where(lo_half, o2[:GRID_W], o2[GRID_W:])

        n_jobs = len(jobs)
        scores, probs, outs = {}, {}, []
        for tick in range(n_jobs + 2 * NATTEN_PIPE_DEPTH):
            if tick < n_jobs:
                scores[tick] = score(jobs[tick])
            j = tick - NATTEN_PIPE_DEPTH
            if 0 <= j < n_jobs:
                probs[j] = softmax(scores.pop(j))
            j = tick - 2 * NATTEN_PIPE_DEPTH
            if 0 <= j < n_jobs:
                outs.append(weighted_values(probs.pop(j), jobs[j]))

        pairs = NA_HEADS // 2
        for sub in range(NATTEN_ROWS_PER_ITER):
            q0 = jobs[sub * pairs][0]
            o_ref[0, pl.ds(q0, GRID_W), :] = jnp.concatenate(
                outs[sub * pairs:(sub + 1) * pairs], axis=1).astype(o_ref.dtype)
        return carry

    lax.fori_loop(0, NATTEN_ROWS_PER_STEP // NATTEN_ROWS_PER_ITER, row_group, 0)


def _natten(q, k, v, table):
    bsz, s, _ = q.shape
    rows = s // GRID_W
    assert rows >= NA_KH and rows % NATTEN_ROWS_PER_STEP == 0
    tq = NATTEN_ROWS_PER_STEP * GRID_W
    return pl.pallas_call(
        functools.partial(_natten_kernel, rows=rows),
        grid=(bsz, rows // NATTEN_ROWS_PER_STEP),
        in_specs=[
            pl.BlockSpec((1, tq, D_ATTN), lambda b, i: (b, i, 0)),
            pl.BlockSpec((1, s, D_ATTN), lambda b, i: (b, 0, 0)),
            pl.BlockSpec((1, s, D_ATTN), lambda b, i: (b, 0, 0)),
            pl.BlockSpec(table.shape, lambda b, i: (0, 0, 0, 0)),
        ],
        out_specs=pl.BlockSpec((1, tq, D_ATTN), lambda b, i: (b, i, 0)),
        out_shape=jax.ShapeDtypeStruct((bsz, s, D_ATTN), BF16),
        compiler_params=pltpu.CompilerParams(
            dimension_semantics=("arbitrary", "arbitrary"), vmem_limit_bytes=VMEM_LIMIT_BYTES),
        name="natten",
    )(q, k, v, table)


def _mixer_kernel(x_ref, a_ref, u_ref, up_ref, un_ref, g1_ref, wg_ref, bg_ref, cw_ref, cb_ref,
                  lg_ref, lb_ref, wna_ref, wco_ref, wo_ref, pin_ref, o_ref, d_ref, y_ref, u2_ref,
                  gates_ref, h_ref, *, tiles_per_seq):
    tm = x_ref.shape[0]
    run = tm // SUBLANES
    pad = CONV_K // 2
    n_lb = CONV_CH // LANES
    t = pl.program_id(0) % tiles_per_seq
    prev = up_ref[...].astype(F32) * (t != 0).astype(F32)
    nxt = un_ref[...].astype(F32) * (t != tiles_per_seq - 1).astype(F32)

    d_main = jnp.dot(pin_ref[...], u_ref[...], preferred_element_type=F32)
    sub = lax.broadcasted_iota(jnp.int32, (SUBLANES, LANES), 0)
    vreg = lambda v, cols: d_main[v * SUBLANES:(v + 1) * SUBLANES, cols]
    for lb in range(n_lb):
        cols = slice(lb * LANES, (lb + 1) * LANES)
        d_ref[lb, pad * SUBLANES:pad * SUBLANES + tm, :] = d_main[:, cols]
        for j in range(pad):
            lo = jnp.where(sub == 0, prev[CONV_HALO - pad + j:CONV_HALO - pad + j + 1, cols],
                           pltpu.roll(vreg(run - pad + j, cols), 1, axis=0))
            d_ref[lb, j * SUBLANES:(j + 1) * SUBLANES, :] = lo
            hi = jnp.where(sub == SUBLANES - 1, nxt[j:j + 1, cols],
                           pltpu.roll(vreg(j, cols), SUBLANES - 1, axis=0))
            d_ref[lb, (pad + run + j) * SUBLANES:(pad + run + j + 1) * SUBLANES, :] = hi

    x = x_ref[...]
    h_ref[...] = (_rms_scale(x) * g1_ref[...]).astype(BF16)

    n_acc = CONV_VREG_BLOCK
    blocks_per_lb = run // n_acc
    n_chunks = gates_ref.shape[0]
    blocks_per_chunk = n_lb * blocks_per_lb // n_chunks

    def conv_block(blk):
        lb = blk // blocks_per_lb
        row0 = pl.multiple_of((blk % blocks_per_lb) * (n_acc * SUBLANES), n_acc * SUBLANES)
        w = [cw_ref[lb, kk:kk + 1, :] for kk in range(CONV_K)]
        accs = [cb_ref[lb] + jnp.zeros((SUBLANES, LANES), F32)] * n_acc
        for i in range(n_acc + CONV_K - 1):
            dv = d_ref[lb, pl.ds(row0 + i * SUBLANES, SUBLANES), :]
            for j in range(n_acc):
                if 0 <= i - j < CONV_K:
                    accs[j] = accs[j] + w[i - j] * dv
        return lb, row0, accs

    def chunk(c, carry):
        convs = [conv_block(c * blocks_per_chunk + b) for b in range(blocks_per_chunk)]
        g = jnp.dot(h_ref[...], wg_ref[c], preferred_element_type=F32) + bg_ref[c]
        for lb, row0, accs in convs:
            for j in range(n_acc):
                y_ref[lb, pl.ds(row0 + j * SUBLANES, SUBLANES), :] = accs[j]
        gates_ref[c] = g
        return carry

    lax.fori_loop(0, n_chunks, chunk, 0)

    half = n_chunks // 2
    gate = lambda first: jax.nn.sigmoid(
        jnp.concatenate([gates_ref[first + n] for n in range(half)], axis=1))
    gated_a = gate(0) * jnp.dot(a_ref[...], wna_ref[...], preferred_element_type=F32)

    y = jnp.concatenate([y_ref[lb] for lb in range(n_lb)], axis=1)
    mu = jnp.mean(y, axis=-1, keepdims=True)
    dlt = y - mu
    var = jnp.mean(dlt * dlt, axis=-1, keepdims=True)
    z = dlt * lax.rsqrt(var + EPS) * lg_ref[...] + lb_ref[...]
    z = z * jax.nn.sigmoid(z)
    for lb in range(n_lb):
        y_ref[lb] = z[:, lb * LANES:(lb + 1) * LANES]
    for lb in range(n_lb):
        for s in range(SUBLANES):
            for v0 in range(0, run, BF16_SUBLANES):
                u2_ref[s * run + v0:s * run + v0 + BF16_SUBLANES, lb * LANES:(lb + 1) * LANES] = (
                    y_ref[lb, pl.ds(v0 * SUBLANES + s, BF16_SUBLANES, stride=SUBLANES), :].astype(BF16))

    br_b = jnp.dot(u2_ref[...], wco_ref[...], preferred_element_type=F32)
    merged = gated_a + gate(half) * br_b
    o_ref[...] = x + jnp.dot(merged.astype(BF16), wo_ref[...], preferred_element_type=F32)


def _conv_layout_selector(tm):
    run = tm // SUBLANES
    v, s = np.meshgrid(np.arange(run), np.arange(SUBLANES), indexing="ij")
    p_in = np.zeros((tm, tm), np.float32)
    p_in[(v * SUBLANES + s).ravel(), (s * run + v).ravel()] = 1.0
    return jnp.asarray(p_in, BF16)


def _mixer(x2d, a2d, u2d, seq, g1, wg, bg, cw, cb, lg, lb, wna, wco, wo):
    m = x2d.shape[0]
    tm = TM_MIXER
    run = tm // SUBLANES
    assert seq % tm == 0 and run % BF16_SUBLANES == 0 and run % CONV_VREG_BLOCK == 0
    tiles_per_seq = seq // tm
    hb = tm // CONV_HALO
    n_hb = m // CONV_HALO
    n_lb = CONV_CH // LANES
    n_chunks = 2 * D_MODEL // GATE_CHUNK
    assert (n_lb * (run // CONV_VREG_BLOCK)) % n_chunks == 0
    wg = wg.reshape(D_MODEL, n_chunks, GATE_CHUNK).transpose(1, 0, 2)
    bg = bg.reshape(n_chunks, 1, GATE_CHUNK)
    cw = cw.reshape(CONV_K, n_lb, LANES).transpose(1, 0, 2)
    cb = cb.reshape(n_lb, 1, LANES)
    p_in = _conv_layout_selector(tm)
    full = lambda arr: pl.BlockSpec(arr.shape, lambda i: (0,) * arr.ndim)
    return pl.pallas_call(
        functools.partial(_mixer_kernel, tiles_per_seq=tiles_per_seq),
        grid=(m // tm,),
        in_specs=[
            pl.BlockSpec((tm, D_MODEL), lambda i: (i, 0)),
            pl.BlockSpec((tm, D_ATTN), lambda i: (i, 0)),
            pl.BlockSpec((tm, CONV_CH), lambda i: (i, 0)),
            pl.BlockSpec((CONV_HALO, CONV_CH), lambda i: (jnp.maximum(i * hb - 1, 0), 0)),
            pl.BlockSpec((CONV_HALO, CONV_CH), lambda i: (jnp.minimum((i + 1) * hb, n_hb - 1), 0)),
            full(g1), full(wg), full(bg), full(cw), full(cb), full(lg), full(lb),
            full(wna), full(wco), full(wo), full(p_in),
        ],
        out_specs=pl.BlockSpec((tm, D_MODEL), lambda i: (i, 0)),
        out_shape=jax.ShapeDtypeStruct((m, D_MODEL), F32),
        scratch_shapes=[
            pltpu.VMEM((n_lb, tm + 2 * (CONV_K // 2) * SUBLANES, LANES), F32),
            pltpu.VMEM((n_lb, tm, LANES), F32),
            pltpu.VMEM((tm, CONV_CH), BF16),
            pltpu.VMEM((n_chunks, tm, GATE_CHUNK), F32),
            pltpu.VMEM((tm, D_MODEL), BF16),
        ],
        compiler_params=pltpu.CompilerParams(
            dimension_semantics=("arbitrary",), vmem_limit_bytes=VMEM_LIMIT_BYTES),
        name="mixer",
    )(x2d, a2d, u2d, u2d, u2d, g1, wg, bg, cw, cb, lg, lb, wna, wco, wo, p_in)


def _ffn_kernel(x_ref, xp_ref, xn_ref, g2_ref, wup_ref, cw_ref, cb_ref, wdn_ref, gf_ref, o_ref,
                act_ref, xs_ref, *, tiles_per_seq, final_norm):
    assert FFN_CONV_K == 3 and FFN_HALO == SUBLANES
    ts = x_ref.shape[0] // FFN_SPLIT
    n_lb = D_MODEL // LANES
    group = SUBLANES * SUBLANES
    n_groups = ts // group
    t = pl.program_id(0) % tiles_per_seq
    lanes = lambda lb: slice(lb * LANES, (lb + 1) * LANES)
    sub = lax.broadcasted_iota(jnp.int32, (SUBLANES, FFN_CHUNK), 0)
    g2 = g2_ref[...]

    def regroup(rows, slot):
        for lb in range(n_lb):
            xs_ref[slot, lb] = rows[:, lanes(lb)]
        return jnp.concatenate([
            jnp.concatenate([xs_ref[slot, lb, pl.ds(k * SUBLANES - (k % SUBLANES) * (SUBLANES - 1),
                                                    SUBLANES, stride=SUBLANES), :]
                             for lb in range(n_lb)], axis=1)
            for k in range(ts // SUBLANES)], axis=0)

    norm = lambda rows: _rms_scale(rows) * g2

    def prologue(slot):
        r0 = slot * ts
        before = (norm(xp_ref[...]) * (t != 0).astype(F32) if slot == 0
                  else norm(x_ref[r0 - FFN_HALO:r0, :]))
        after = (norm(xn_ref[...]) * (t != tiles_per_seq - 1).astype(F32) if slot == FFN_SPLIT - 1
                 else norm(x_ref[r0 + ts:r0 + ts + FFN_HALO, :]))
        x = regroup(x_ref[r0:r0 + ts, :], slot)
        h_ext = jnp.concatenate([
            (x * g2).astype(BF16), jnp.concatenate([before, after], axis=0).astype(BF16)], axis=0)
        r_ext = jnp.concatenate([_rms_inv(x), jnp.ones((2 * FFN_HALO, 1), F32)], axis=0)
        return x, (h_ext, r_ext)

    def up_chunk(slot, lhs, c):
        h_ext, r_ext = lhs

        def conv_up(col0):
            cols = slice(col0, col0 + FFN_CHUNK)
            up = r_ext * jnp.dot(h_ext, wup_ref[:, cols], preferred_element_type=F32)
            vreg = lambda k: up[k * SUBLANES:(k + 1) * SUBLANES]
            first = lambda g: vreg(g * SUBLANES)
            last = lambda g: vreg(g * SUBLANES + SUBLANES - 1)
            halo_prev, halo_next = vreg(ts // SUBLANES), vreg(ts // SUBLANES + 1)
            down = [pltpu.roll(halo_prev, 1, axis=0)] + [
                pltpu.roll(last(g), 1, axis=0) for g in range(n_groups)]
            up_ = [pltpu.roll(first(g), SUBLANES - 1, axis=0) for g in range(n_groups)] + [
                pltpu.roll(halo_next, SUBLANES - 1, axis=0)]
            prev, nxt = [], []
            for g in range(n_groups):
                lo, hi = g * group, (g + 1) * group
                prev += [jnp.where(sub == 0, down[g], down[g + 1]), up[lo:hi - SUBLANES]]
                nxt += [up[lo + SUBLANES:hi], jnp.where(sub == SUBLANES - 1, up_[g + 1], up_[g])]
            return (cb_ref[:, cols] + cw_ref[0:1, cols] * jnp.concatenate(prev, axis=0)
                    + cw_ref[1:2, cols] * up[:ts] + cw_ref[2:3, cols] * jnp.concatenate(nxt, axis=0))

        gate = conv_up(c * FFN_CHUNK)
        val = conv_up(D_FF + c * FFN_CHUNK)
        act_ref[slot, :, c * FFN_CHUNK:(c + 1) * FFN_CHUNK] = (
            _twice_gelu_exact(gate) * val).astype(BF16)

    def down(slot, x):
        y = x + jnp.dot(act_ref[slot], wdn_ref[...], preferred_element_type=F32)
        return _rms_scale(y) * gf_ref[...] if final_norm else y

    def epilogue(slot, y):
        o_ref[slot * ts:(slot + 1) * ts, :] = regroup(y, slot)

    for slot in range(FFN_SPLIT):
        x, h_ext = prologue(slot)
        for c in range(D_FF // FFN_CHUNK):
            up_chunk(slot, h_ext, c)
        epilogue(slot, down(slot, x))


def _ffn(x2d, seq, g2, wup, cw, cb, wdn, gf, final_norm):
    m = x2d.shape[0]
    tm = TM_FFN
    assert seq % tm == 0
    tiles_per_seq = seq // tm
    hb = tm // FFN_HALO
    n_hb = m // FFN_HALO
    const = lambda i: (0, 0)
    full = lambda arr: pl.BlockSpec(arr.shape, const)
    resident = lambda arr: pl.BlockSpec(arr.shape, const, pipeline_mode=pl.Buffered(1))
    return pl.pallas_call(
        functools.partial(_ffn_kernel, tiles_per_seq=tiles_per_seq, final_norm=final_norm),
        grid=(m // tm,),
        in_specs=[
            pl.BlockSpec((tm, D_MODEL), lambda i: (i, 0)),
            pl.BlockSpec((FFN_HALO, D_MODEL), lambda i: (jnp.maximum(i * hb - 1, 0), 0)),
            pl.BlockSpec((FFN_HALO, D_MODEL), lambda i: (jnp.minimum((i + 1) * hb, n_hb - 1), 0)),
            full(g2), resident(wup), full(cw), full(cb), resident(wdn), full(gf),
        ],
        out_specs=pl.BlockSpec((tm, D_MODEL), lambda i: (i, 0)),
        out_shape=jax.ShapeDtypeStruct((m, D_MODEL), F32),
        scratch_shapes=[
            pltpu.VMEM((FFN_SPLIT, tm // FFN_SPLIT, D_FF), BF16),
            pltpu.VMEM((FFN_SPLIT, D_MODEL // LANES, tm // FFN_SPLIT, LANES), F32),
        ],
        compiler_params=pltpu.CompilerParams(
            dimension_semantics=("arbitrary",), vmem_limit_bytes=VMEM_LIMIT_BYTES),
        name="ffn",
    )(x2d, x2d, x2d, g2, wup, cw, cb, wdn, gf)


def kernel(x, norm1_g, w_in, b_in, rpb, w_na_out, conv_dw_w, conv_dw_b, conv_ln_g, conv_ln_b,
           w_conv_out, w_out, norm2_g, w_up, ffn_dw_w, ffn_dw_b, w_down, norm_f_g):
    bsz, seq, d = x.shape
    depth = w_in.shape[0]
    n_tok = 3 * D_ATTN + 2 * CONV_CH
    row = lambda v: v.reshape(1, -1).astype(F32)
    x2d = x.reshape(bsz * seq, d)
    for l in range(depth):
        w_in_l = w_in[l].astype(BF16)
        b_in_l = row(b_in[l])
        g1 = row(norm1_g[l])
        q, k, v, u = _inproj(x2d, g1, w_in_l[:, :n_tok], b_in_l[:, :n_tok])
        shp = (bsz, seq, D_ATTN)
        a = _natten(q.reshape(shp), k.reshape(shp), v.reshape(shp), _natten_bias_table(rpb[l]))
        x2d = _mixer(
            x2d, a.reshape(bsz * seq, D_ATTN), u, seq, g1, w_in_l[:, n_tok:], b_in_l[:, n_tok:],
            conv_dw_w[l].astype(F32), row(conv_dw_b[l]), row(conv_ln_g[l]), row(conv_ln_b[l]),
            w_na_out[l].astype(BF16), w_conv_out[l].astype(BF16), w_out[l].astype(BF16))
        x2d = _ffn(
            x2d, seq, row(norm2_g[l]), w_up[l].astype(BF16), ffn_dw_w[l].astype(F32),
            row(ffn_dw_b[l]), (0.5 * w_down[l]).astype(BF16), row(norm_f_g),
            final_norm=(l == depth - 1))
    return x2d.reshape(bsz, seq, d)
```

```python
import functools

import jax
import jax.numpy as jnp
import numpy as np
from jax import lax
from jax.experimental import pallas as pl
from jax.experimental.pallas import tpu as pltpu

F32 = jnp.float32
BF16 = jnp.bfloat16

D_MODEL = 1024
GRID_W = 64
NA_HEADS = 8
NA_HEAD_DIM = 64
D_ATTN = NA_HEADS * NA_HEAD_DIM
NA_KH = 8
NA_KW = 16
CONV_CH = 512
CONV_K = 31
D_FF = 2816
FFN_CONV_K = 3
EPS = 1e-6
NEG = -1e30
LOG2_E = 1.4426950408889634

LANES = 128
SUBLANES = 8
BF16_SUBLANES = 16
VMEM_LIMIT_BYTES = 56 * 1024 * 1024

TM_INPROJ = 1024
TM_MIXER = 512
TM_FFN = 512
FFN_SPLIT = 1
NATTEN_ROWS_PER_STEP = 16
NATTEN_ROWS_PER_ITER = 16
NATTEN_PIPE_LAG = 3
CONV_VREG_BLOCK = 8
GATE_CHUNK = 1024
CONV_HALO = 16
FFN_HALO = 8
FFN_CHUNK = 256


def _rms_inv(x):
    return lax.rsqrt(jnp.mean(x * x, axis=-1, keepdims=True) + EPS)


def _rms_scale(x):
    return x * _rms_inv(x)


def _twice_gelu_exact(x):
    return x * (1.0 + lax.erf(x * np.float32(np.sqrt(0.5))))


def _inproj_kernel(x_ref, g_ref, w_ref, b_ref, q_ref, k_ref, v_ref, u_ref):
    h = (_rms_scale(x_ref[...]) * g_ref[...]).astype(BF16)

    def proj(n):
        cols = slice(n * D_ATTN, (n + 1) * D_ATTN)
        return jnp.dot(h, w_ref[:, cols], preferred_element_type=F32) + b_ref[:, cols]

    q_ref[...] = (proj(0) * (NA_HEAD_DIM ** -0.5 * LOG2_E)).astype(BF16)
    k_ref[...] = proj(1).astype(BF16)
    v_ref[...] = proj(2).astype(BF16)
    u_ref[...] = (proj(3) * jax.nn.sigmoid(proj(4))).astype(BF16)


def _inproj(x2d, g, w, b):
    m = x2d.shape[0]
    tm = TM_INPROJ
    n_in = w.shape[1]
    const = lambda i: (0, 0)
    out = jax.ShapeDtypeStruct((m, D_ATTN), BF16)
    row_spec = pl.BlockSpec((tm, D_ATTN), lambda i: (i, 0))
    return pl.pallas_call(
        _inproj_kernel,
        grid=(m // tm,),
        in_specs=[
            pl.BlockSpec((tm, D_MODEL), lambda i: (i, 0)),
            pl.BlockSpec((1, D_MODEL), const),
            pl.BlockSpec((D_MODEL, n_in), const),
            pl.BlockSpec((1, n_in), const),
        ],
        out_specs=[row_spec, row_spec, row_spec, row_spec],
        out_shape=[out, out, out, out],
        compiler_params=pltpu.CompilerParams(
            dimension_semantics=("arbitrary",), vmem_limit_bytes=VMEM_LIMIT_BYTES),
        name="inproj",
    )(x2d, g, w, b)


def _natten_bias_table(rpb):
    col = jnp.arange(GRID_W)
    col_start = jnp.clip(col - NA_KW // 2, 0, GRID_W - NA_KW)
    kc = col[None, :]
    in_win = (kc >= col_start[:, None]) & (kc < col_start[:, None] + NA_KW)
    reach = GRID_W - NA_KW
    padded = jnp.pad(rpb.astype(F32) * LOG2_E, ((0, 0), (0, 0), (reach, reach)))
    t = jnp.stack([padded[:, :, GRID_W - 1 - c:2 * GRID_W - 1 - c] for c in range(GRID_W)], axis=2)
    t = jnp.where(in_win[None, None], t, NEG)
    nxt = jnp.concatenate([t[:, 1:], jnp.full_like(t[:, :1], NEG)], axis=1)
    return jnp.concatenate([t, nxt], axis=-1)


def _natten_kernel(q_ref, k_ref, v_ref, tb_ref, o_ref, *, rows):
    step = pl.program_id(1)
    lane = lax.broadcasted_iota(jnp.int32, (GRID_W, LANES), 1)
    lo_half = lane < NA_HEAD_DIM
    n_keys = NA_KH * GRID_W

    def row_group(it, carry):
        jobs = []
        for sub in range(NATTEN_ROWS_PER_ITER):
            rr = it * NATTEN_ROWS_PER_ITER + sub
            r = step * NATTEN_ROWS_PER_STEP + rr
            rs = jnp.clip(r - NA_KH // 2, 0, rows - NA_KH)
            base = rs - r + NA_KH - 1
            q0 = pl.multiple_of(rr * GRID_W, GRID_W)
            k0 = pl.multiple_of(rs * GRID_W, GRID_W)
            jobs += [(q0, k0, base, p) for p in range(NA_HEADS // 2)]

        def score(job):
            q0, k0, base, p = job
            cols = slice(p * LANES, (p + 1) * LANES)
            qp = q_ref[0, pl.ds(q0, GRID_W), cols]
            kp = k_ref[0, pl.ds(k0, n_keys), cols]
            zero = jnp.zeros_like(qp)
            q2 = jnp.concatenate([jnp.where(lo_half, qp, zero), jnp.where(lo_half, zero, qp)], axis=0)
            s = lax.dot_general(q2, kp, (((1,), (1,)), ((), ())), preferred_element_type=F32)
            bias = jnp.concatenate(
                [jnp.concatenate([tb_ref[2 * p + hh, base + 2 * j] for j in range(NA_KH // 2)], axis=1)
                 for hh in range(2)], axis=0)
            return s + bias

        def softmax(s):
            e = jnp.exp2(s - jnp.max(s, axis=-1, keepdims=True))
            return e.astype(BF16), jnp.sum(e, axis=-1, keepdims=True)

        def weighted_values(prob, job):
            e, l = prob
            q0, k0, base, p = job
            vp = v_ref[0, pl.ds(k0, n_keys), p * LANES:(p + 1) * LANES]
            o2 = jnp.dot(e, vp, preferred_element_type=F32) / l
            return jnp.where(lo_half, o2[:GRID_W], o2[GRID_W:])

        n_jobs = len(jobs)
        scores, outs = {}, []
        for tick in range(n_jobs + NATTEN_PIPE_LAG):
            if tick < n_jobs:
                scores[tick] = score(jobs[tick])
            j = tick - NATTEN_PIPE_LAG
            if 0 <= j < n_jobs:
                outs.append(weighted_values(softmax(scores.pop(j)), jobs[j]))

        pairs = NA_HEADS // 2
        for sub in range(NATTEN_ROWS_PER_ITER):
            q0 = jobs[sub * pairs][0]
            o_ref[0, pl.ds(q0, GRID_W), :] = jnp.concatenate(
                outs[sub * pairs:(sub + 1) * pairs], axis=1).astype(o_ref.dtype)
        return carry

    lax.fori_loop(0, NATTEN_ROWS_PER_STEP // NATTEN_ROWS_PER_ITER, row_group, 0)


def _natten(q, k, v, table):
    bsz, s, _ = q.shape
    rows = s // GRID_W
    assert rows >= NA_KH and rows % NATTEN_ROWS_PER_STEP == 0
    tq = NATTEN_ROWS_PER_STEP * GRID_W
    return pl.pallas_call(
        functools.partial(_natten_kernel, rows=rows),
        grid=(bsz, rows // NATTEN_ROWS_PER_STEP),
        in_specs=[
            pl.BlockSpec((1, tq, D_ATTN), lambda b, i: (b, i, 0)),
            pl.BlockSpec((1, s, D_ATTN), lambda b, i: (b, 0, 0)),
            pl.BlockSpec((1, s, D_ATTN), lambda b, i: (b, 0, 0)),
            pl.BlockSpec(table.shape, lambda b, i: (0, 0, 0, 0)),
        ],
        out_specs=pl.BlockSpec((1, tq, D_ATTN), lambda b, i: (b, i, 0)),
        out_shape=jax.ShapeDtypeStruct((bsz, s, D_ATTN), BF16),
        compiler_params=pltpu.CompilerParams(
            dimension_semantics=("arbitrary", "arbitrary"), vmem_limit_bytes=VMEM_LIMIT_BYTES),
        name="natten",
    )(q, k, v, table)


def _mixer_kernel(x_ref, a_ref, u_ref, up_ref, un_ref, g1_ref, wg_ref, bg_ref, cw_ref, cb_ref,
                  lg_ref, lb_ref, wna_ref, wco_ref, wo_ref, pin_ref, o_ref, d_ref, y_ref, u2_ref,
                  gates_ref, h_ref, *, tiles_per_seq):
    tm = x_ref.shape[0]
    run = tm // SUBLANES
    pad = CONV_K // 2
    n_lb = CONV_CH // LANES
    t = pl.program_id(0) % tiles_per_seq
    prev = up_ref[...].astype(F32) * (t != 0).astype(F32)
    nxt = un_ref[...].astype(F32) * (t != tiles_per_seq - 1).astype(F32)

    d_main = jnp.dot(pin_ref[...], u_ref[...], preferred_element_type=F32)
    sub = lax.broadcasted_iota(jnp.int32, (SUBLANES, LANES), 0)
    vreg = lambda v, cols: d_main[v * SUBLANES:(v + 1) * SUBLANES, cols]
    for lb in range(n_lb):
        cols = slice(lb * LANES, (lb + 1) * LANES)
        d_ref[lb, pad * SUBLANES:pad * SUBLANES + tm, :] = d_main[:, cols]
        for j in range(pad):
            lo = jnp.where(sub == 0, prev[CONV_HALO - pad + j:CONV_HALO - pad + j + 1, cols],
                           pltpu.roll(vreg(run - pad + j, cols), 1, axis=0))
            d_ref[lb, j * SUBLANES:(j + 1) * SUBLANES, :] = lo
            hi = jnp.where(sub == SUBLANES - 1, nxt[j:j + 1, cols],
                           pltpu.roll(vreg(j, cols), SUBLANES - 1, axis=0))
            d_ref[lb, (pad + run + j) * SUBLANES:(pad + run + j + 1) * SUBLANES, :] = hi

    x = x_ref[...]
    h_ref[...] = (_rms_scale(x) * g1_ref[...]).astype(BF16)

    n_acc = CONV_VREG_BLOCK
    blocks_per_lb = run // n_acc
    n_chunks = gates_ref.shape[0]
    blocks_per_chunk = n_lb * blocks_per_lb // n_chunks

    def conv_block(blk):
        lb = blk // blocks_per_lb
        row0 = pl.multiple_of((blk % blocks_per_lb) * (n_acc * SUBLANES), n_acc * SUBLANES)
        w = [cw_ref[lb, kk:kk + 1, :] for kk in range(CONV_K)]
        accs = [cb_ref[lb] + jnp.zeros((SUBLANES, LANES), F32)] * n_acc
        for i in range(n_acc + CONV_K - 1):
            dv = d_ref[lb, pl.ds(row0 + i * SUBLANES, SUBLANES), :]
            for j in range(n_acc):
                if 0 <= i - j < CONV_K:
                    accs[j] = accs[j] + w[i - j] * dv
        return lb, row0, accs

    def chunk(c, carry):
        convs = [conv_block(c * blocks_per_chunk + b) for b in range(blocks_per_chunk)]
        g = jnp.dot(h_ref[...], wg_ref[c], preferred_element_type=F32) + bg_ref[c]
        for lb, row0, accs in convs:
            for j in range(n_acc):
                y_ref[lb, pl.ds(row0 + j * SUBLANES, SUBLANES), :] = accs[j]
        gates_ref[c] = g
        return carry

    lax.fori_loop(0, n_chunks, chunk, 0)

    half = n_chunks // 2
    gate = lambda first: jax.nn.sigmoid(
        jnp.concatenate([gates_ref[first + n] for n in range(half)], axis=1))
    gated_a = gate(0) * jnp.dot(a_ref[...], wna_ref[...], preferred_element_type=F32)

    y = jnp.concatenate([y_ref[lb] for lb in range(n_lb)], axis=1)
    mu = jnp.mean(y, axis=-1, keepdims=True)
    dlt = y - mu
    var = jnp.mean(dlt * dlt, axis=-1, keepdims=True)
    z = dlt * lax.rsqrt(var + EPS) * lg_ref[...] + lb_ref[...]
    z = z * jax.nn.sigmoid(z)
    for lb in range(n_lb):
        y_ref[lb] = z[:, lb * LANES:(lb + 1) * LANES]
    for lb in range(n_lb):
        for s in range(SUBLANES):
            for v0 in range(0, run, BF16_SUBLANES):
                u2_ref[s * run + v0:s * run + v0 + BF16_SUBLANES, lb * LANES:(lb + 1) * LANES] = (
                    y_ref[lb, pl.ds(v0 * SUBLANES + s, BF16_SUBLANES, stride=SUBLANES), :].astype(BF16))

    br_b = jnp.dot(u2_ref[...], wco_ref[...], preferred_element_type=F32)
    merged = gated_a + gate(half) * br_b
    o_ref[...] = x + jnp.dot(merged.astype(BF16), wo_ref[...], preferred_element_type=F32)


def _conv_layout_selector(tm):
    run = tm // SUBLANES
    v, s = np.meshgrid(np.arange(run), np.arange(SUBLANES), indexing="ij")
    p_in = np.zeros((tm, tm), np.float32)
    p_in[(v * SUBLANES + s).ravel(), (s * run + v).ravel()] = 1.0
    return jnp.asarray(p_in, BF16)


def _mixer(x2d, a2d, u2d, seq, g1, wg, bg, cw, cb, lg, lb, wna, wco, wo):
    m = x2d.shape[0]
    tm = TM_MIXER
    run = tm // SUBLANES
    assert seq % tm == 0 and run % BF16_SUBLANES == 0 and run % CONV_VREG_BLOCK == 0
    tiles_per_seq = seq // tm
    hb = tm // CONV_HALO
    n_hb = m // CONV_HALO
    n_lb = CONV_CH // LANES
    n_chunks = 2 * D_MODEL // GATE_CHUNK
    assert (n_lb * (run // CONV_VREG_BLOCK)) % n_chunks == 0
    wg = wg.reshape(D_MODEL, n_chunks, GATE_CHUNK).transpose(1, 0, 2)
    bg = bg.reshape(n_chunks, 1, GATE_CHUNK)
    cw = cw.reshape(CONV_K, n_lb, LANES).transpose(1, 0, 2)
    cb = cb.reshape(n_lb, 1, LANES)
    p_in = _conv_layout_selector(tm)
    full = lambda arr: pl.BlockSpec(arr.shape, lambda i: (0,) * arr.ndim)
    return pl.pallas_call(
        functools.partial(_mixer_kernel, tiles_per_seq=tiles_per_seq),
        grid=(m // tm,),
        in_specs=[
            pl.BlockSpec((tm, D_MODEL), lambda i: (i, 0)),
            pl.BlockSpec((tm, D_ATTN), lambda i: (i, 0)),
            pl.BlockSpec((tm, CONV_CH), lambda i: (i, 0)),
            pl.BlockSpec((CONV_HALO, CONV_CH), lambda i: (jnp.maximum(i * hb - 1, 0), 0)),
            pl.BlockSpec((CONV_HALO, CONV_CH), lambda i: (jnp.minimum((i + 1) * hb, n_hb - 1), 0)),
            full(g1), full(wg), full(bg), full(cw), full(cb), full(lg), full(lb),
            full(wna), full(wco), full(wo), full(p_in),
        ],
        out_specs=pl.BlockSpec((tm, D_MODEL), lambda i: (i, 0)),
        out_shape=jax.ShapeDtypeStruct((m, D_MODEL), F32),
        scratch_shapes=[
            pltpu.VMEM((n_lb, tm + 2 * (CONV_K // 2) * SUBLANES, LANES), F32),
            pltpu.VMEM((n_lb, tm, LANES), F32),
            pltpu.VMEM((tm, CONV_CH), BF16),
            pltpu.VMEM((n_chunks, tm, GATE_CHUNK), F32),
            pltpu.VMEM((tm, D_MODEL), BF16),
        ],
        compiler_params=pltpu.CompilerParams(
            dimension_semantics=("arbitrary",), vmem_limit_bytes=VMEM_LIMIT_BYTES),
        name="mixer",
    )(x2d, a2d, u2d, u2d, u2d, g1, wg, bg, cw, cb, lg, lb, wna, wco, wo, p_in)


def _ffn_kernel(x_ref, xp_ref, xn_ref, g2_ref, wup_ref, cw_ref, cb_ref, wdn_ref, gf_ref, o_ref,
                act_ref, xs_ref, *, tiles_per_seq, final_norm):
    assert FFN_CONV_K == 3 and FFN_HALO == SUBLANES
    ts = x_ref.shape[0] // FFN_SPLIT
    n_lb = D_MODEL // LANES
    group = SUBLANES * SUBLANES
    n_groups = ts // group
    t = pl.program_id(0) % tiles_per_seq
    lanes = lambda lb: slice(lb * LANES, (lb + 1) * LANES)
    sub = lax.broadcasted_iota(jnp.int32, (SUBLANES, FFN_CHUNK), 0)
    g2 = g2_ref[...]

    def regroup(rows, slot):
        for lb in range(n_lb):
            xs_ref[slot, lb] = rows[:, lanes(lb)]
        return jnp.concatenate([
            jnp.concatenate([xs_ref[slot, lb, pl.ds(k * SUBLANES - (k % SUBLANES) * (SUBLANES - 1),
                                                    SUBLANES, stride=SUBLANES), :]
                             for lb in range(n_lb)], axis=1)
            for k in range(ts // SUBLANES)], axis=0)

    norm = lambda rows: _rms_scale(rows) * g2

    def prologue(slot):
        r0 = slot * ts
        before = (norm(xp_ref[...]) * (t != 0).astype(F32) if slot == 0
                  else norm(x_ref[r0 - FFN_HALO:r0, :]))
        after = (norm(xn_ref[...]) * (t != tiles_per_seq - 1).astype(F32) if slot == FFN_SPLIT - 1
                 else norm(x_ref[r0 + ts:r0 + ts + FFN_HALO, :]))
        x = regroup(x_ref[r0:r0 + ts, :], slot)
        h_ext = jnp.concatenate([
            (x * g2).astype(BF16), jnp.concatenate([before, after], axis=0).astype(BF16)], axis=0)
        r_ext = jnp.concatenate([_rms_inv(x), jnp.ones((2 * FFN_HALO, 1), F32)], axis=0)
        return x, (h_ext, r_ext)

    def up_chunk(slot, lhs, c):
        h_ext, r_ext = lhs

        def conv_up(col0):
            cols = slice(col0, col0 + FFN_CHUNK)
            up = r_ext * jnp.dot(h_ext, wup_ref[:, cols], preferred_element_type=F32)
            vreg = lambda k: up[k * SUBLANES:(k + 1) * SUBLANES]
            first = lambda g: vreg(g * SUBLANES)
            last = lambda g: vreg(g * SUBLANES + SUBLANES - 1)
            halo_prev, halo_next = vreg(ts // SUBLANES), vreg(ts // SUBLANES + 1)
            down = [pltpu.roll(halo_prev, 1, axis=0)] + [
                pltpu.roll(last(g), 1, axis=0) for g in range(n_groups)]
            up_ = [pltpu.roll(first(g), SUBLANES - 1, axis=0) for g in range(n_groups)] + [
                pltpu.roll(halo_next, SUBLANES - 1, axis=0)]
            prev, nxt = [], []
            for g in range(n_groups):
                lo, hi = g * group, (g + 1) * group
                prev += [jnp.where(sub == 0, down[g], down[g + 1]), up[lo:hi - SUBLANES]]
                nxt += [up[lo + SUBLANES:hi], jnp.where(sub == SUBLANES - 1, up_[g + 1], up_[g])]
            return (cb_ref[:, cols] + cw_ref[0:1, cols] * jnp.concatenate(prev, axis=0)
                    + cw_ref[1:2, cols] * up[:ts] + cw_ref[2:3, cols] * jnp.concatenate(nxt, axis=0))

        gate = conv_up(c * FFN_CHUNK)
        val = conv_up(D_FF + c * FFN_CHUNK)
        act_ref[slot, :, c * FFN_CHUNK:(c + 1) * FFN_CHUNK] = (
            _twice_gelu_exact(gate) * val).astype(BF16)

    def down(slot, x):
        y = x + jnp.dot(act_ref[slot], wdn_ref[...], preferred_element_type=F32)
        return _rms_scale(y) * gf_ref[...] if final_norm else y

    def epilogue(slot, y):
        o_ref[slot * ts:(slot + 1) * ts, :] = regroup(y, slot)

    for slot in range(FFN_SPLIT):
        x, h_ext = prologue(slot)
        for c in range(D_FF // FFN_CHUNK):
            up_chunk(slot, h_ext, c)
        epilogue(slot, down(slot, x))


def _ffn(x2d, seq, g2, wup, cw, cb, wdn, gf, final_norm):
    m = x2d.shape[0]
    tm = TM_FFN
    assert seq % tm == 0
    tiles_per_seq = seq // tm
    hb = tm // FFN_HALO
    n_hb = m // FFN_HALO
    const = lambda i: (0, 0)
    full = lambda arr: pl.BlockSpec(arr.shape, const)
    resident = lambda arr: pl.BlockSpec(arr.shape, const, pipeline_mode=pl.Buffered(1))
    return pl.pallas_call(
        functools.partial(_ffn_kernel, tiles_per_seq=tiles_per_seq, final_norm=final_norm),
        grid=(m // tm,),
        in_specs=[
            pl.BlockSpec((tm, D_MODEL), lambda i: (i, 0)),
            pl.BlockSpec((FFN_HALO, D_MODEL), lambda i: (jnp.maximum(i * hb - 1, 0), 0)),
            pl.BlockSpec((FFN_HALO, D_MODEL), lambda i: (jnp.minimum((i + 1) * hb, n_hb - 1), 0)),
            full(g2), resident(wup), full(cw), full(cb), resident(wdn), full(gf),
        ],
        out_specs=pl.BlockSpec((tm, D_MODEL), lambda i: (i, 0)),
        out_shape=jax.ShapeDtypeStruct((m, D_MODEL), F32),
        scratch_shapes=[
            pltpu.VMEM((FFN_SPLIT, tm // FFN_SPLIT, D_FF), BF16),
            pltpu.VMEM((FFN_SPLIT, D_MODEL // LANES, tm // FFN_SPLIT, LANES), F32),
        ],
        compiler_params=pltpu.CompilerParams(
            dimension_semantics=("arbitrary",), vmem_limit_bytes=VMEM_LIMIT_BYTES),
        name="ffn",
    )(x2d, x2d, x2d, g2, wup, cw, cb, wdn, gf)


def kernel(x, norm1_g, w_in, b_in, rpb, w_na_out, conv_dw_w, conv_dw_b, conv_ln_g, conv_ln_b,
           w_conv_out, w_out, norm2_g, w_up, ffn_dw_w, ffn_dw_b, w_down, norm_f_g):
    bsz, seq, d = x.shape
    depth = w_in.shape[0]
    n_tok = 3 * D_ATTN + 2 * CONV_CH
    row = lambda v: v.reshape(1, -1).astype(F32)
    x2d = x.reshape(bsz * seq, d)
    for l in range(depth):
        w_in_l = w_in[l].astype(BF16)
        b_in_l = row(b_in[l])
        g1 = row(norm1_g[l])
        q, k, v, u = _inproj(x2d, g1, w_in_l[:, :n_tok], b_in_l[:, :n_tok])
        shp = (bsz, seq, D_ATTN)
        a = _natten(q.reshape(shp), k.reshape(shp), v.reshape(shp), _natten_bias_table(rpb[l]))
        x2d = _mixer(
            x2d, a.reshape(bsz * seq, D_ATTN), u, seq, g1, w_in_l[:, n_tok:], b_in_l[:, n_tok:],
            conv_dw_w[l].astype(F32), row(conv_dw_b[l]), row(conv_ln_g[l]), row(conv_ln_b[l]),
            w_na_out[l].astype(BF16), w_conv_out[l].astype(BF16), w_out[l].astype(BF16))
        x2d = _ffn(
            x2d, seq, row(norm2_g[l]), w_up[l].astype(BF16), ffn_dw_w[l].astype(F32),
            row(ffn_dw_b[l]), (0.5 * w_down[l]).astype(BF16), row(norm_f_g),
            final_norm=(l == depth - 1))
    return x2d.reshape(bsz, seq, d)
```

```python
import functools

import jax
import jax.numpy as jnp
import numpy as np
from jax import lax
from jax.experimental import pallas as pl
from jax.experimental.pallas import tpu as pltpu

F32 = jnp.float32
BF16 = jnp.bfloat16

D_MODEL = 1024
GRID_W = 64
NA_HEADS = 8
NA_HEAD_DIM = 64
D_ATTN = NA_HEADS * NA_HEAD_DIM
NA_KH = 8
NA_KW = 16
CONV_CH = 512
CONV_K = 31
D_FF = 2816
FFN_CONV_K = 3
EPS = 1e-6
NEG = -1e30
LOG2_E = 1.4426950408889634

LANES = 128
SUBLANES = 8
BF16_SUBLANES = 16
VMEM_LIMIT_BYTES = 56 * 1024 * 1024

TM_INPROJ = 1024
TM_MIXER = 512
TM_FFN = 512
NATTEN_ROWS_PER_STEP = 16
NATTEN_ROWS_PER_ITER = 16
NATTEN_PIPE_LAG = 3
CONV_VREG_BLOCK = 8
GATE_CHUNK = 1024
CONV_HALO = 16
FFN_HALO = 8
FFN_CHUNK = 256


def _rms_inv(x):
    return lax.rsqrt(jnp.mean(x * x, axis=-1, keepdims=True) + EPS)


def _rms_scale(x):
    return x * _rms_inv(x)


def _twice_gelu_exact(x):
    return x * (1.0 + lax.erf(x * np.float32(np.sqrt(0.5))))


def _inproj_kernel(x_ref, g_ref, w_ref, b_ref, q_ref, k_ref, v_ref, u_ref):
    h = (_rms_scale(x_ref[...]) * g_ref[...]).astype(BF16)

    def proj(n):
        cols = slice(n * D_ATTN, (n + 1) * D_ATTN)
        return jnp.dot(h, w_ref[:, cols], preferred_element_type=F32) + b_ref[:, cols]

    q_ref[...] = (proj(0) * (NA_HEAD_DIM ** -0.5 * LOG2_E)).astype(BF16)
    k_ref[...] = proj(1).astype(BF16)
    v_ref[...] = proj(2).astype(BF16)
    u_ref[...] = (proj(3) * jax.nn.sigmoid(proj(4))).astype(BF16)


def _inproj(x2d, g, w, b):
    m = x2d.shape[0]
    tm = TM_INPROJ
    n_in = w.shape[1]
    const = lambda i: (0, 0)
    out = jax.ShapeDtypeStruct((m, D_ATTN), BF16)
    row_spec = pl.BlockSpec((tm, D_ATTN), lambda i: (i, 0))
    return pl.pallas_call(
        _inproj_kernel,
        grid=(m // tm,),
        in_specs=[
            pl.BlockSpec((tm, D_MODEL), lambda i: (i, 0)),
            pl.BlockSpec((1, D_MODEL), const),
            pl.BlockSpec((D_MODEL, n_in), const),
            pl.BlockSpec((1, n_in), const),
        ],
        out_specs=[row_spec, row_spec, row_spec, row_spec],
        out_shape=[out, out, out, out],
        compiler_params=pltpu.CompilerParams(
            dimension_semantics=("arbitrary",), vmem_limit_bytes=VMEM_LIMIT_BYTES),
        name="inproj",
    )(x2d, g, w, b)


def _natten_bias_table(rpb):
    col = jnp.arange(GRID_W)
    col_start = jnp.clip(col - NA_KW // 2, 0, GRID_W - NA_KW)
    kc = col[None, :]
    in_win = (kc >= col_start[:, None]) & (kc < col_start[:, None] + NA_KW)
    reach = GRID_W - NA_KW
    padded = jnp.pad(rpb.astype(F32) * LOG2_E, ((0, 0), (0, 0), (reach, reach)))
    t = jnp.stack([padded[:, :, GRID_W - 1 - c:2 * GRID_W - 1 - c] for c in range(GRID_W)], axis=2)
    t = jnp.where(in_win[None, None], t, NEG)
    nxt = jnp.concatenate([t[:, 1:], jnp.full_like(t[:, :1], NEG)], axis=1)
    return jnp.concatenate([t, nxt], axis=-1)


def _natten_kernel(q_ref, k_ref, v_ref, tb_ref, o_ref, *, rows):
    step = pl.program_id(1)
    lane = lax.broadcasted_iota(jnp.int32, (GRID_W, LANES), 1)
    lo_half = lane < NA_HEAD_DIM
    n_keys = NA_KH * GRID_W

    def row_group(it, carry):
        jobs = []
        for sub in range(NATTEN_ROWS_PER_ITER):
            rr = it * NATTEN_ROWS_PER_ITER + sub
            r = step * NATTEN_ROWS_PER_STEP + rr
            rs = jnp.clip(r - NA_KH // 2, 0, rows - NA_KH)
            base = rs - r + NA_KH - 1
            q0 = pl.multiple_of(rr * GRID_W, GRID_W)
            k0 = pl.multiple_of(rs * GRID_W, GRID_W)
            jobs += [(q0, k0, base, p) for p in range(NA_HEADS // 2)]

        def score(job):
            q0, k0, base, p = job
            cols = slice(p * LANES, (p + 1) * LANES)
            qp = q_ref[0, pl.ds(q0, GRID_W), cols]
            kp = k_ref[0, pl.ds(k0, n_keys), cols]
            zero = jnp.zeros_like(qp)
            q2 = jnp.concatenate([jnp.where(lo_half, qp, zero), jnp.where(lo_half, zero, qp)], axis=0)
            s = lax.dot_general(q2, kp, (((1,), (1,)), ((), ())), preferred_element_type=F32)
            bias = jnp.concatenate(
                [jnp.concatenate([tb_ref[2 * p + hh, base + 2 * j] for j in range(NA_KH // 2)], axis=1)
                 for hh in range(2)], axis=0)
            return s + bias

        def softmax(s):
            e = jnp.exp2(s - jnp.max(s, axis=-1, keepdims=True))
            return e.astype(BF16), jnp.sum(e, axis=-1, keepdims=True)

        def weighted_values(prob, job):
            e, l = prob
            q0, k0, base, p = job
            vp = v_ref[0, pl.ds(k0, n_keys), p * LANES:(p + 1) * LANES]
            o2 = jnp.dot(e, vp, preferred_element_type=F32) / l
            return jnp.where(lo_half, o2[:GRID_W], o2[GRID_W:])

        n_jobs = len(jobs)
        scores, outs = {}, []
        for tick in range(n_jobs + NATTEN_PIPE_LAG):
            if tick < n_jobs:
                scores[tick] = score(jobs[tick])
            j = tick - NATTEN_PIPE_LAG
            if 0 <= j < n_jobs:
                outs.append(weighted_values(softmax(scores.pop(j)), jobs[j]))

        pairs = NA_HEADS // 2
        for sub in range(NATTEN_ROWS_PER_ITER):
            q0 = jobs[sub * pairs][0]
            o_ref[0, pl.ds(q0, GRID_W), :] = jnp.concatenate(
                outs[sub * pairs:(sub + 1) * pairs], axis=1).astype(o_ref.dtype)
        return carry

    lax.fori_loop(0, NATTEN_ROWS_PER_STEP // NATTEN_ROWS_PER_ITER, row_group, 0)


def _natten(q, k, v, table):
    bsz, s, _ = q.shape
    rows = s // GRID_W
    assert rows >= NA_KH and rows % NATTEN_ROWS_PER_STEP == 0
    tq = NATTEN_ROWS_PER_STEP * GRID_W
    return pl.pallas_call(
        functools.partial(_natten_kernel, rows=rows),
        grid=(bsz, rows // NATTEN_ROWS_PER_STEP),
        in_specs=[
            pl.BlockSpec((1, tq, D_ATTN), lambda b, i: (b, i, 0)),
            pl.BlockSpec((1, s, D_ATTN), lambda b, i: (b, 0, 0)),
            pl.BlockSpec((1, s, D_ATTN), lambda b, i: (b, 0, 0)),
            pl.BlockSpec(table.shape, lambda b, i: (0, 0, 0, 0)),
        ],
        out_specs=pl.BlockSpec((1, tq, D_ATTN), lambda b, i: (b, i, 0)),
        out_shape=jax.ShapeDtypeStruct((bsz, s, D_ATTN), BF16),
        compiler_params=pltpu.CompilerParams(
            dimension_semantics=("arbitrary", "arbitrary"), vmem_limit_bytes=VMEM_LIMIT_BYTES),
        name="natten",
    )(q, k, v, table)


def _mixer_kernel(x_ref, a_ref, u_ref, up_ref, un_ref, g1_ref, wg_ref, bg_ref, cw_ref, cb_ref,
                  lg_ref, lb_ref, wna_ref, wco_ref, wo_ref, pin_ref, o_ref, d_ref, y_ref, u2_ref,
                  gates_ref, h_ref, *, tiles_per_seq):
    tm = x_ref.shape[0]
    run = tm // SUBLANES
    pad = CONV_K // 2
    n_lb = CONV_CH // LANES
    t = pl.program_id(0) % tiles_per_seq
    prev = up_ref[...].astype(F32) * (t != 0).astype(F32)
    nxt = un_ref[...].astype(F32) * (t != tiles_per_seq - 1).astype(F32)

    d_main = jnp.dot(pin_ref[...], u_ref[...], preferred_element_type=F32)
    sub = lax.broadcasted_iota(jnp.int32, (SUBLANES, LANES), 0)
    vreg = lambda v, cols: d_main[v * SUBLANES:(v + 1) * SUBLANES, cols]
    for lb in range(n_lb):
        cols = slice(lb * LANES, (lb + 1) * LANES)
        d_ref[lb, pad * SUBLANES:pad * SUBLANES + tm, :] = d_main[:, cols]
        for j in range(pad):
            lo = jnp.where(sub == 0, prev[CONV_HALO - pad + j:CONV_HALO - pad + j + 1, cols],
                           pltpu.roll(vreg(run - pad + j, cols), 1, axis=0))
            d_ref[lb, j * SUBLANES:(j + 1) * SUBLANES, :] = lo
            hi = jnp.where(sub == SUBLANES - 1, nxt[j:j + 1, cols],
                           pltpu.roll(vreg(j, cols), SUBLANES - 1, axis=0))
            d_ref[lb, (pad + run + j) * SUBLANES:(pad + run + j + 1) * SUBLANES, :] = hi

    x = x_ref[...]
    h_ref[...] = (_rms_scale(x) * g1_ref[...]).astype(BF16)

    n_acc = CONV_VREG_BLOCK
    blocks_per_lb = run // n_acc
    n_chunks = gates_ref.shape[0]
    blocks_per_chunk = n_lb * blocks_per_lb // n_chunks

    def conv_block(blk):
        lb = blk // blocks_per_lb
        row0 = pl.multiple_of((blk % blocks_per_lb) * (n_acc * SUBLANES), n_acc * SUBLANES)
        w = [cw_ref[lb, kk:kk + 1, :] for kk in range(CONV_K)]
        accs = [cb_ref[lb] + jnp.zeros((SUBLANES, LANES), F32)] * n_acc
        for i in range(n_acc + CONV_K - 1):
            dv = d_ref[lb, pl.ds(row0 + i * SUBLANES, SUBLANES), :]
            for j in range(n_acc):
                if 0 <= i - j < CONV_K:
                    accs[j] = accs[j] + w[i - j] * dv
        return lb, row0, accs

    def chunk(c, carry):
        convs = [conv_block(c * blocks_per_chunk + b) for b in range(blocks_per_chunk)]
        g = jnp.dot(h_ref[...], wg_ref[c], preferred_element_type=F32) + bg_ref[c]
        for lb, row0, accs in convs:
            for j in range(n_acc):
                y_ref[lb, pl.ds(row0 + j * SUBLANES, SUBLANES), :] = accs[j]
        gates_ref[c] = g
        return carry

    lax.fori_loop(0, n_chunks, chunk, 0)

    half = n_chunks // 2
    gate = lambda first: jax.nn.sigmoid(
        jnp.concatenate([gates_ref[first + n] for n in range(half)], axis=1))
    gated_a = gate(0) * jnp.dot(a_ref[...], wna_ref[...], preferred_element_type=F32)

    y = jnp.concatenate([y_ref[lb] for lb in range(n_lb)], axis=1)
    mu = jnp.mean(y, axis=-1, keepdims=True)
    dlt = y - mu
    var = jnp.mean(dlt * dlt, axis=-1, keepdims=True)
    z = dlt * lax.rsqrt(var + EPS) * lg_ref[...] + lb_ref[...]
    z = z * jax.nn.sigmoid(z)
    for lb in range(n_lb):
        y_ref[lb] = z[:, lb * LANES:(lb + 1) * LANES]
    for lb in range(n_lb):
        for s in range(SUBLANES):
            for v0 in range(0, run, BF16_SUBLANES):
                u2_ref[s * run + v0:s * run + v0 + BF16_SUBLANES, lb * LANES:(lb + 1) * LANES] = (
                    y_ref[lb, pl.ds(v0 * SUBLANES + s, BF16_SUBLANES, stride=SUBLANES), :].astype(BF16))

    br_b = jnp.dot(u2_ref[...], wco_ref[...], preferred_element_type=F32)
    merged = gated_a + gate(half) * br_b
    o_ref[...] = x + jnp.dot(merged.astype(BF16), wo_ref[...], preferred_element_type=F32)


def _conv_layout_selector(tm):
    run = tm // SUBLANES
    v, s = np.meshgrid(np.arange(run), np.arange(SUBLANES), indexing="ij")
    p_in = np.zeros((tm, tm), np.float32)
    p_in[(v * SUBLANES + s).ravel(), (s * run + v).ravel()] = 1.0
    return jnp.asarray(p_in, BF16)


def _mixer(x2d, a2d, u2d, seq, g1, wg, bg, cw, cb, lg, lb, wna, wco, wo):
    m = x2d.shape[0]
    tm = TM_MIXER
    run = tm // SUBLANES
    assert seq % tm == 0 and run % BF16_SUBLANES == 0 and run % CONV_VREG_BLOCK == 0
    tiles_per_seq = seq // tm
    hb = tm // CONV_HALO
    n_hb = m // CONV_HALO
    n_lb = CONV_CH // LANES
    n_chunks = 2 * D_MODEL // GATE_CHUNK
    assert (n_lb * (run // CONV_VREG_BLOCK)) % n_chunks == 0
    wg = wg.reshape(D_MODEL, n_chunks, GATE_CHUNK).transpose(1, 0, 2)
    bg = bg.reshape(n_chunks, 1, GATE_CHUNK)
    cw = cw.reshape(CONV_K, n_lb, LANES).transpose(1, 0, 2)
    cb = cb.reshape(n_lb, 1, LANES)
    p_in = _conv_layout_selector(tm)
    full = lambda arr: pl.BlockSpec(arr.shape, lambda i: (0,) * arr.ndim)
    return pl.pallas_call(
        functools.partial(_mixer_kernel, tiles_per_seq=tiles_per_seq),
        grid=(m // tm,),
        in_specs=[
            pl.BlockSpec((tm, D_MODEL), lambda i: (i, 0)),
            pl.BlockSpec((tm, D_ATTN), lambda i: (i, 0)),
            pl.BlockSpec((tm, CONV_CH), lambda i: (i, 0)),
            pl.BlockSpec((CONV_HALO, CONV_CH), lambda i: (jnp.maximum(i * hb - 1, 0), 0)),
            pl.BlockSpec((CONV_HALO, CONV_CH), lambda i: (jnp.minimum((i + 1) * hb, n_hb - 1), 0)),
            full(g1), full(wg), full(bg), full(cw), full(cb), full(lg), full(lb),
            full(wna), full(wco), full(wo), full(p_in),
        ],
        out_specs=pl.BlockSpec((tm, D_MODEL), lambda i: (i, 0)),
        out_shape=jax.ShapeDtypeStruct((m, D_MODEL), F32),
        scratch_shapes=[
            pltpu.VMEM((n_lb, tm + 2 * (CONV_K // 2) * SUBLANES, LANES), F32),
            pltpu.VMEM((n_lb, tm, LANES), F32),
            pltpu.VMEM((tm, CONV_CH), BF16),
            pltpu.VMEM((n_chunks, tm, GATE_CHUNK), F32),
            pltpu.VMEM((tm, D_MODEL), BF16),
        ],
        compiler_params=pltpu.CompilerParams(
            dimension_semantics=("arbitrary",), vmem_limit_bytes=VMEM_LIMIT_BYTES),
        name="mixer",
    )(x2d, a2d, u2d, u2d, u2d, g1, wg, bg, cw, cb, lg, lb, wna, wco, wo, p_in)


def _ffn_kernel(x_ref, xp_ref, xn_ref, g2_ref, wup_ref, cw_ref, cb_ref, wdn_ref, gf_ref, o_ref,
                act_ref, xs_ref, *, tiles_per_seq, final_norm):
    assert FFN_CONV_K == 3 and FFN_HALO == SUBLANES
    ts = x_ref.shape[0]
    n_lb = D_MODEL // LANES
    group = SUBLANES * SUBLANES
    n_groups = ts // group
    t = pl.program_id(0) % tiles_per_seq
    lanes = lambda lb: slice(lb * LANES, (lb + 1) * LANES)
    sub = lax.broadcasted_iota(jnp.int32, (SUBLANES, FFN_CHUNK), 0)
    g2 = g2_ref[...]

    def regroup(rows):
        for lb in range(n_lb):
            xs_ref[lb] = rows[:, lanes(lb)]
        return jnp.concatenate([
            jnp.concatenate([xs_ref[lb, pl.ds(k * SUBLANES - (k % SUBLANES) * (SUBLANES - 1),
                                              SUBLANES, stride=SUBLANES), :]
                             for lb in range(n_lb)], axis=1)
            for k in range(ts // SUBLANES)], axis=0)

    norm = lambda rows: _rms_scale(rows) * g2

    def prologue():
        before = norm(xp_ref[...]) * (t != 0).astype(F32)
        after = norm(xn_ref[...]) * (t != tiles_per_seq - 1).astype(F32)
        x = regroup(x_ref[...])
        h_ext = jnp.concatenate([
            (x * g2).astype(BF16), jnp.concatenate([before, after], axis=0).astype(BF16)], axis=0)
        r_ext = jnp.concatenate([_rms_inv(x), jnp.ones((2 * FFN_HALO, 1), F32)], axis=0)
        return x, (h_ext, r_ext)

    def up_chunk(lhs, c):
        h_ext, r_ext = lhs

        def conv_up(col0):
            cols = slice(col0, col0 + FFN_CHUNK)
            up = r_ext * jnp.dot(h_ext, wup_ref[:, cols], preferred_element_type=F32)
            vreg = lambda k: up[k * SUBLANES:(k + 1) * SUBLANES]
            first = lambda g: vreg(g * SUBLANES)
            last = lambda g: vreg(g * SUBLANES + SUBLANES - 1)
            halo_prev, halo_next = vreg(ts // SUBLANES), vreg(ts // SUBLANES + 1)
            down = [pltpu.roll(halo_prev, 1, axis=0)] + [
                pltpu.roll(last(g), 1, axis=0) for g in range(n_groups)]
            up_ = [pltpu.roll(first(g), SUBLANES - 1, axis=0) for g in range(n_groups)] + [
                pltpu.roll(halo_next, SUBLANES - 1, axis=0)]
            prev, nxt = [], []
            for g in range(n_groups):
                lo, hi = g * group, (g + 1) * group
                prev += [jnp.where(sub == 0, down[g], down[g + 1]), up[lo:hi - SUBLANES]]
                nxt += [up[lo + SUBLANES:hi], jnp.where(sub == SUBLANES - 1, up_[g + 1], up_[g])]
            return (cb_ref[:, cols] + cw_ref[0:1, cols] * jnp.concatenate(prev, axis=0)
                    + cw_ref[1:2, cols] * up[:ts] + cw_ref[2:3, cols] * jnp.concatenate(nxt, axis=0))

        gate = conv_up(c * FFN_CHUNK)
        val = conv_up(D_FF + c * FFN_CHUNK)
        act_ref[:, c * FFN_CHUNK:(c + 1) * FFN_CHUNK] = (_twice_gelu_exact(gate) * val).astype(BF16)

    x, lhs = prologue()
    for c in range(D_FF // FFN_CHUNK):
        up_chunk(lhs, c)
    y = x + jnp.dot(act_ref[...], wdn_ref[...], preferred_element_type=F32)
    if final_norm:
        y = _rms_scale(y) * gf_ref[...]
    o_ref[...] = regroup(y)


def _ffn(x2d, seq, g2, wup, cw, cb, wdn, gf, final_norm):
    m = x2d.shape[0]
    tm = TM_FFN
    assert seq % tm == 0
    tiles_per_seq = seq // tm
    hb = tm // FFN_HALO
    n_hb = m // FFN_HALO
    const = lambda i: (0, 0)
    full = lambda arr: pl.BlockSpec(arr.shape, const)
    resident = lambda arr: pl.BlockSpec(arr.shape, const, pipeline_mode=pl.Buffered(1))
    return pl.pallas_call(
        functools.partial(_ffn_kernel, tiles_per_seq=tiles_per_seq, final_norm=final_norm),
        grid=(m // tm,),
        in_specs=[
            pl.BlockSpec((tm, D_MODEL), lambda i: (i, 0)),
            pl.BlockSpec((FFN_HALO, D_MODEL), lambda i: (jnp.maximum(i * hb - 1, 0), 0)),
            pl.BlockSpec((FFN_HALO, D_MODEL), lambda i: (jnp.minimum((i + 1) * hb, n_hb - 1), 0)),
            full(g2), resident(wup), full(cw), full(cb), resident(wdn), full(gf),
        ],
        out_specs=pl.BlockSpec((tm, D_MODEL), lambda i: (i, 0)),
        out_shape=jax.ShapeDtypeStruct((m, D_MODEL), F32),
        scratch_shapes=[
            pltpu.VMEM((tm, D_FF), BF16),
            pltpu.VMEM((D_MODEL // LANES, tm, LANES), F32),
        ],
        compiler_params=pltpu.CompilerParams(
            dimension_semantics=("arbitrary",), vmem_limit_bytes=VMEM_LIMIT_BYTES),
        name="ffn",
    )(x2d, x2d, x2d, g2, wup, cw, cb, wdn, gf)


def kernel(x, norm1_g, w_in, b_in, rpb, w_na_out, conv_dw_w, conv_dw_b, conv_ln_g, conv_ln_b,
           w_conv_out, w_out, norm2_g, w_up, ffn_dw_w, ffn_dw_b, w_down, norm_f_g):
    bsz, seq, d = x.shape
    depth = w_in.shape[0]
    n_tok = 3 * D_ATTN + 2 * CONV_CH
    row = lambda v: v.reshape(1, -1).astype(F32)
    x2d = x.reshape(bsz * seq, d)
    for l in range(depth):
        w_in_l = w_in[l].astype(BF16)
        b_in_l = row(b_in[l])
        g1 = row(norm1_g[l])
        q, k, v, u = _inproj(x2d, g1, w_in_l[:, :n_tok], b_in_l[:, :n_tok])
        shp = (bsz, seq, D_ATTN)
        a = _natten(q.reshape(shp), k.reshape(shp), v.reshape(shp), _natten_bias_table(rpb[l]))
        x2d = _mixer(
            x2d, a.reshape(bsz * seq, D_ATTN), u, seq, g1, w_in_l[:, n_tok:], b_in_l[:, n_tok:],
            conv_dw_w[l].astype(F32), row(conv_dw_b[l]), row(conv_ln_g[l]), row(conv_ln_b[l]),
            w_na_out[l].astype(BF16), w_conv_out[l].astype(BF16), w_out[l].astype(BF16))
        x2d = _ffn(
            x2d, seq, row(norm2_g[l]), w_up[l].astype(BF16), ffn_dw_w[l].astype(F32),
            row(ffn_dw_b[l]), (0.5 * w_down[l]).astype(BF16), row(norm_f_g),
            final_norm=(l == depth - 1))
    return x2d.reshape(bsz, seq, d)
```

```python
import functools

import jax
import jax.numpy as jnp
import numpy as np
from jax import lax
from jax.experimental import pallas as pl
from jax.experimental.pallas import tpu as pltpu

F32 = jnp.float32
BF16 = jnp.bfloat16

D_MODEL = 1024
GRID_W = 64
NA_HEADS = 8
NA_HEAD_DIM = 64
D_ATTN = NA_HEADS * NA_HEAD_DIM
NA_KH = 8
NA_KW = 16
CONV_CH = 512
CONV_K = 31
D_FF = 2816
FFN_CONV_K = 3
EPS = 1e-6
NEG = -1e30
LOG2_E = 1.4426950408889634

LANES = 128
SUBLANES = 8
BF16_SUBLANES = 16
VMEM_LIMIT_BYTES = 56 * 1024 * 1024

TM_INPROJ = 1024
TM_MIXER = 512
TM_FFN = 512
NATTEN_ROWS_PER_STEP = 16
NATTEN_ROWS_PER_ITER = 16
NATTEN_PIPE_LAG = 3
CONV_VREG_BLOCK = 8
GATE_CHUNK = 1024
CONV_HALO = 16
FFN_HALO = 8
FFN_CHUNK = 256


def _sigmoid(x):
    return 0.5 * jnp.tanh(0.5 * x) + 0.5


def _rms_inv(x):
    return lax.rsqrt(jnp.mean(x * x, axis=-1, keepdims=True) + EPS)


def _rms_scale(x):
    return x * _rms_inv(x)


def _twice_gelu_exact(x):
    return x * (1.0 + lax.erf(x * np.float32(np.sqrt(0.5))))


def _inproj_kernel(x_ref, g_ref, w_ref, b_ref, q_ref, k_ref, v_ref, u_ref):
    h = (_rms_scale(x_ref[...]) * g_ref[...]).astype(BF16)

    def proj(n):
        cols = slice(n * D_ATTN, (n + 1) * D_ATTN)
        return jnp.dot(h, w_ref[:, cols], preferred_element_type=F32) + b_ref[:, cols]

    q_ref[...] = (proj(0) * (NA_HEAD_DIM ** -0.5 * LOG2_E)).astype(BF16)
    k_ref[...] = proj(1).astype(BF16)
    v_ref[...] = proj(2).astype(BF16)
    u_ref[...] = (proj(3) * _sigmoid(proj(4))).astype(BF16)


def _inproj(x2d, g, w, b):
    m = x2d.shape[0]
    tm = TM_INPROJ
    n_in = w.shape[1]
    const = lambda i: (0, 0)
    out = jax.ShapeDtypeStruct((m, D_ATTN), BF16)
    row_spec = pl.BlockSpec((tm, D_ATTN), lambda i: (i, 0))
    return pl.pallas_call(
        _inproj_kernel,
        grid=(m // tm,),
        in_specs=[
            pl.BlockSpec((tm, D_MODEL), lambda i: (i, 0)),
            pl.BlockSpec((1, D_MODEL), const),
            pl.BlockSpec((D_MODEL, n_in), const),
            pl.BlockSpec((1, n_in), const),
        ],
        out_specs=[row_spec, row_spec, row_spec, row_spec],
        out_shape=[out, out, out, out],
        compiler_params=pltpu.CompilerParams(
            dimension_semantics=("arbitrary",), vmem_limit_bytes=VMEM_LIMIT_BYTES),
        name="inproj",
    )(x2d, g, w, b)


def _natten_bias_table(rpb):
    col = jnp.arange(GRID_W)
    col_start = jnp.clip(col - NA_KW // 2, 0, GRID_W - NA_KW)
    kc = col[None, :]
    in_win = (kc >= col_start[:, None]) & (kc < col_start[:, None] + NA_KW)
    reach = GRID_W - NA_KW
    padded = jnp.pad(rpb.astype(F32) * LOG2_E, ((0, 0), (0, 0), (reach, reach)))
    t = jnp.stack([padded[:, :, GRID_W - 1 - c:2 * GRID_W - 1 - c] for c in range(GRID_W)], axis=2)
    t = jnp.where(in_win[None, None], t, NEG)
    nxt = jnp.concatenate([t[:, 1:], jnp.full_like(t[:, :1], NEG)], axis=1)
    return jnp.concatenate([t, nxt], axis=-1)


def _natten_kernel(q_ref, k_ref, v_ref, tb_ref, o_ref, *, rows):
    step = pl.program_id(1)
    lane = lax.broadcasted_iota(jnp.int32, (GRID_W, LANES), 1)
    lo_half = lane < NA_HEAD_DIM
    n_keys = NA_KH * GRID_W

    def row_group(it, carry):
        jobs = []
        for sub in range(NATTEN_ROWS_PER_ITER):
            rr = it * NATTEN_ROWS_PER_ITER + sub
            r = step * NATTEN_ROWS_PER_STEP + rr
            rs = jnp.clip(r - NA_KH // 2, 0, rows - NA_KH)
            base = rs - r + NA_KH - 1
            q0 = pl.multiple_of(rr * GRID_W, GRID_W)
            k0 = pl.multiple_of(rs * GRID_W, GRID_W)
            jobs += [(q0, k0, base, p) for p in range(NA_HEADS // 2)]

        def score(job):
            q0, k0, base, p = job
            cols = slice(p * LANES, (p + 1) * LANES)
            qp = q_ref[0, pl.ds(q0, GRID_W), cols]
            kp = k_ref[0, pl.ds(k0, n_keys), cols]
            zero = jnp.zeros_like(qp)
            q2 = jnp.concatenate([jnp.where(lo_half, qp, zero), jnp.where(lo_half, zero, qp)], axis=0)
            s = lax.dot_general(q2, kp, (((1,), (1,)), ((), ())), preferred_element_type=F32)
            bias = jnp.concatenate(
                [jnp.concatenate([tb_ref[2 * p + hh, base + 2 * j] for j in range(NA_KH // 2)], axis=1)
                 for hh in range(2)], axis=0)
            return s + bias

        def softmax(s):
            e = jnp.exp2(s - jnp.max(s, axis=-1, keepdims=True))
            return e.astype(BF16), jnp.sum(e, axis=-1, keepdims=True)

        def weighted_values(prob, job):
            e, l = prob
            q0, k0, base, p = job
            vp = v_ref[0, pl.ds(k0, n_keys), p * LANES:(p + 1) * LANES]
            o2 = jnp.dot(e, vp, preferred_element_type=F32) / l
            return jnp.where(lo_half, o2[:GRID_W], o2[GRID_W:])

        n_jobs = len(jobs)
        scores, outs = {}, []
        for tick in range(n_jobs + NATTEN_PIPE_LAG):
            if tick < n_jobs:
                scores[tick] = score(jobs[tick])
            j = tick - NATTEN_PIPE_LAG
            if 0 <= j < n_jobs:
                outs.append(weighted_values(softmax(scores.pop(j)), jobs[j]))

        pairs = NA_HEADS // 2
        for sub in range(NATTEN_ROWS_PER_ITER):
            q0 = jobs[sub * pairs][0]
            o_ref[0, pl.ds(q0, GRID_W), :] = jnp.concatenate(
                outs[sub * pairs:(sub + 1) * pairs], axis=1).astype(o_ref.dtype)
        return carry

    lax.fori_loop(0, NATTEN_ROWS_PER_STEP // NATTEN_ROWS_PER_ITER, row_group, 0)


def _natten(q, k, v, table):
    bsz, s, _ = q.shape
    rows = s // GRID_W
    assert rows >= NA_KH and rows % NATTEN_ROWS_PER_STEP == 0
    tq = NATTEN_ROWS_PER_STEP * GRID_W
    return pl.pallas_call(
        functools.partial(_natten_kernel, rows=rows),
        grid=(bsz, rows // NATTEN_ROWS_PER_STEP),
        in_specs=[
            pl.BlockSpec((1, tq, D_ATTN), lambda b, i: (b, i, 0)),
            pl.BlockSpec((1, s, D_ATTN), lambda b, i: (b, 0, 0)),
            pl.BlockSpec((1, s, D_ATTN), lambda b, i: (b, 0, 0)),
            pl.BlockSpec(table.shape, lambda b, i: (0, 0, 0, 0)),
        ],
        out_specs=pl.BlockSpec((1, tq, D_ATTN), lambda b, i: (b, i, 0)),
        out_shape=jax.ShapeDtypeStruct((bsz, s, D_ATTN), BF16),
        compiler_params=pltpu.CompilerParams(
            dimension_semantics=("arbitrary", "arbitrary"), vmem_limit_bytes=VMEM_LIMIT_BYTES),
        name="natten",
    )(q, k, v, table)


def _mixer_kernel(x_ref, a_ref, u_ref, up_ref, un_ref, g1_ref, wg_ref, bg_ref, cw_ref, cb_ref,
                  lg_ref, lb_ref, wna_ref, wco_ref, wo_ref, pin_ref, o_ref, d_ref, y_ref, u2_ref,
                  gates_ref, h_ref, *, tiles_per_seq):
    tm = x_ref.shape[0]
    run = tm // SUBLANES
    pad = CONV_K // 2
    n_lb = CONV_CH // LANES
    t = pl.program_id(0) % tiles_per_seq
    prev = up_ref[...].astype(F32) * (t != 0).astype(F32)
    nxt = un_ref[...].astype(F32) * (t != tiles_per_seq - 1).astype(F32)

    d_main = jnp.dot(pin_ref[...], u_ref[...], preferred_element_type=F32)
    sub = lax.broadcasted_iota(jnp.int32, (SUBLANES, LANES), 0)
    vreg = lambda v, cols: d_main[v * SUBLANES:(v + 1) * SUBLANES, cols]
    for lb in range(n_lb):
        cols = slice(lb * LANES, (lb + 1) * LANES)
        d_ref[lb, pad * SUBLANES:pad * SUBLANES + tm, :] = d_main[:, cols]
        for j in range(pad):
            lo = jnp.where(sub == 0, prev[CONV_HALO - pad + j:CONV_HALO - pad + j + 1, cols],
                           pltpu.roll(vreg(run - pad + j, cols), 1, axis=0))
            d_ref[lb, j * SUBLANES:(j + 1) * SUBLANES, :] = lo
            hi = jnp.where(sub == SUBLANES - 1, nxt[j:j + 1, cols],
                           pltpu.roll(vreg(j, cols), SUBLANES - 1, axis=0))
            d_ref[lb, (pad + run + j) * SUBLANES:(pad + run + j + 1) * SUBLANES, :] = hi

    x = x_ref[...]
    h_ref[...] = (_rms_scale(x) * g1_ref[...]).astype(BF16)

    n_acc = CONV_VREG_BLOCK
    blocks_per_lb = run // n_acc
    n_chunks = gates_ref.shape[0]
    blocks_per_chunk = n_lb * blocks_per_lb // n_chunks

    def conv_block(blk):
        lb = blk // blocks_per_lb
        row0 = pl.multiple_of((blk % blocks_per_lb) * (n_acc * SUBLANES), n_acc * SUBLANES)
        w = [cw_ref[lb, kk:kk + 1, :] for kk in range(CONV_K)]
        accs = [cb_ref[lb] + jnp.zeros((SUBLANES, LANES), F32)] * n_acc
        for i in range(n_acc + CONV_K - 1):
            dv = d_ref[lb, pl.ds(row0 + i * SUBLANES, SUBLANES), :]
            for j in range(n_acc):
                if 0 <= i - j < CONV_K:
                    accs[j] = accs[j] + w[i - j] * dv
        return lb, row0, accs

    def chunk(c, carry):
        convs = [conv_block(c * blocks_per_chunk + b) for b in range(blocks_per_chunk)]
        g = jnp.dot(h_ref[...], wg_ref[c], preferred_element_type=F32) + bg_ref[c]
        for lb, row0, accs in convs:
            for j in range(n_acc):
                y_ref[lb, pl.ds(row0 + j * SUBLANES, SUBLANES), :] = accs[j]
        gates_ref[c] = g
        return carry

    lax.fori_loop(0, n_chunks, chunk, 0)

    half = n_chunks // 2
    gate = lambda first: _sigmoid(
        jnp.concatenate([gates_ref[first + n] for n in range(half)], axis=1))
    gated_a = gate(0) * jnp.dot(a_ref[...], wna_ref[...], preferred_element_type=F32)

    y = jnp.concatenate([y_ref[lb] for lb in range(n_lb)], axis=1)
    mu = jnp.mean(y, axis=-1, keepdims=True)
    dlt = y - mu
    var = jnp.mean(dlt * dlt, axis=-1, keepdims=True)
    z = dlt * lax.rsqrt(var + EPS) * lg_ref[...] + lb_ref[...]
    z = z * _sigmoid(z)
    for lb in range(n_lb):
        y_ref[lb] = z[:, lb * LANES:(lb + 1) * LANES]
    for lb in range(n_lb):
        for s in range(SUBLANES):
            for v0 in range(0, run, BF16_SUBLANES):
                u2_ref[s * run + v0:s * run + v0 + BF16_SUBLANES, lb * LANES:(lb + 1) * LANES] = (
                    y_ref[lb, pl.ds(v0 * SUBLANES + s, BF16_SUBLANES, stride=SUBLANES), :].astype(BF16))

    br_b = jnp.dot(u2_ref[...], wco_ref[...], preferred_element_type=F32)
    merged = gated_a + gate(half) * br_b
    o_ref[...] = x + jnp.dot(merged.astype(BF16), wo_ref[...], preferred_element_type=F32)


def _conv_layout_selector(tm):
    run = tm // SUBLANES
    v, s = np.meshgrid(np.arange(run), np.arange(SUBLANES), indexing="ij")
    p_in = np.zeros((tm, tm), np.float32)
    p_in[(v * SUBLANES + s).ravel(), (s * run + v).ravel()] = 1.0
    return jnp.asarray(p_in, BF16)


def _mixer(x2d, a2d, u2d, seq, g1, wg, bg, cw, cb, lg, lb, wna, wco, wo):
    m = x2d.shape[0]
    tm = TM_MIXER
    run = tm // SUBLANES
    assert seq % tm == 0 and run % BF16_SUBLANES == 0 and run % CONV_VREG_BLOCK == 0
    tiles_per_seq = seq // tm
    hb = tm // CONV_HALO
    n_hb = m // CONV_HALO
    n_lb = CONV_CH // LANES
    n_chunks = 2 * D_MODEL // GATE_CHUNK
    assert (n_lb * (run // CONV_VREG_BLOCK)) % n_chunks == 0
    wg = wg.reshape(D_MODEL, n_chunks, GATE_CHUNK).transpose(1, 0, 2)
    bg = bg.reshape(n_chunks, 1, GATE_CHUNK)
    cw = cw.reshape(CONV_K, n_lb, LANES).transpose(1, 0, 2)
    cb = cb.reshape(n_lb, 1, LANES)
    p_in = _conv_layout_selector(tm)
    full = lambda arr: pl.BlockSpec(arr.shape, lambda i: (0,) * arr.ndim)
    return pl.pallas_call(
        functools.partial(_mixer_kernel, tiles_per_seq=tiles_per_seq),
        grid=(m // tm,),
        in_specs=[
            pl.BlockSpec((tm, D_MODEL), lambda i: (i, 0)),
            pl.BlockSpec((tm, D_ATTN), lambda i: (i, 0)),
            pl.BlockSpec((tm, CONV_CH), lambda i: (i, 0)),
            pl.BlockSpec((CONV_HALO, CONV_CH), lambda i: (jnp.maximum(i * hb - 1, 0), 0)),
            pl.BlockSpec((CONV_HALO, CONV_CH), lambda i: (jnp.minimum((i + 1) * hb, n_hb - 1), 0)),
            full(g1), full(wg), full(bg), full(cw), full(cb), full(lg), full(lb),
            full(wna), full(wco), full(wo), full(p_in),
        ],
        out_specs=pl.BlockSpec((tm, D_MODEL), lambda i: (i, 0)),
        out_shape=jax.ShapeDtypeStruct((m, D_MODEL), F32),
        scratch_shapes=[
            pltpu.VMEM((n_lb, tm + 2 * (CONV_K // 2) * SUBLANES, LANES), F32),
            pltpu.VMEM((n_lb, tm, LANES), F32),
            pltpu.VMEM((tm, CONV_CH), BF16),
            pltpu.VMEM((n_chunks, tm, GATE_CHUNK), F32),
            pltpu.VMEM((tm, D_MODEL), BF16),
        ],
        compiler_params=pltpu.CompilerParams(
            dimension_semantics=("arbitrary",), vmem_limit_bytes=VMEM_LIMIT_BYTES),
        name="mixer",
    )(x2d, a2d, u2d, u2d, u2d, g1, wg, bg, cw, cb, lg, lb, wna, wco, wo, p_in)


def _ffn_kernel(x_ref, xp_ref, xn_ref, g2_ref, wup_ref, cw_ref, cb_ref, wdn_ref, gf_ref, o_ref,
                act_ref, xs_ref, *, tiles_per_seq, final_norm):
    assert FFN_CONV_K == 3 and FFN_HALO == SUBLANES
    ts = x_ref.shape[0]
    n_lb = D_MODEL // LANES
    group = SUBLANES * SUBLANES
    n_groups = ts // group
    t = pl.program_id(0) % tiles_per_seq
    lanes = lambda lb: slice(lb * LANES, (lb + 1) * LANES)
    sub = lax.broadcasted_iota(jnp.int32, (SUBLANES, FFN_CHUNK), 0)
    g2 = g2_ref[...]

    def regroup(rows):
        for lb in range(n_lb):
            xs_ref[lb] = rows[:, lanes(lb)]
        return jnp.concatenate([
            jnp.concatenate([xs_ref[lb, pl.ds(k * SUBLANES - (k % SUBLANES) * (SUBLANES - 1),
                                              SUBLANES, stride=SUBLANES), :]
                             for lb in range(n_lb)], axis=1)
            for k in range(ts // SUBLANES)], axis=0)

    norm = lambda rows: _rms_scale(rows) * g2

    def prologue():
        before = norm(xp_ref[...]) * (t != 0).astype(F32)
        after = norm(xn_ref[...]) * (t != tiles_per_seq - 1).astype(F32)
        x = regroup(x_ref[...])
        h_ext = jnp.concatenate([
            (x * g2).astype(BF16), jnp.concatenate([before, after], axis=0).astype(BF16)], axis=0)
        r_ext = jnp.concatenate([_rms_inv(x), jnp.ones((2 * FFN_HALO, 1), F32)], axis=0)
        return x, (h_ext, r_ext)

    def up_chunk(lhs, c):
        h_ext, r_ext = lhs

        def conv_up(col0):
            cols = slice(col0, col0 + FFN_CHUNK)
            up = r_ext * jnp.dot(h_ext, wup_ref[:, cols], preferred_element_type=F32)
            vreg = lambda k: up[k * SUBLANES:(k + 1) * SUBLANES]
            first = lambda g: vreg(g * SUBLANES)
            last = lambda g: vreg(g * SUBLANES + SUBLANES - 1)
            halo_prev, halo_next = vreg(ts // SUBLANES), vreg(ts // SUBLANES + 1)
            down = [pltpu.roll(halo_prev, 1, axis=0)] + [
                pltpu.roll(last(g), 1, axis=0) for g in range(n_groups)]
            up_ = [pltpu.roll(first(g), SUBLANES - 1, axis=0) for g in range(n_groups)] + [
                pltpu.roll(halo_next, SUBLANES - 1, axis=0)]
            prev, nxt = [], []
            for g in range(n_groups):
                lo, hi = g * group, (g + 1) * group
                prev += [jnp.where(sub == 0, down[g], down[g + 1]), up[lo:hi - SUBLANES]]
                nxt += [up[lo + SUBLANES:hi], jnp.where(sub == SUBLANES - 1, up_[g + 1], up_[g])]
            return (cb_ref[:, cols] + cw_ref[0:1, cols] * jnp.concatenate(prev, axis=0)
                    + cw_ref[1:2, cols] * up[:ts] + cw_ref[2:3, cols] * jnp.concatenate(nxt, axis=0))

        gate = conv_up(c * FFN_CHUNK)
        val = conv_up(D_FF + c * FFN_CHUNK)
        act_ref[:, c * FFN_CHUNK:(c + 1) * FFN_CHUNK] = (_twice_gelu_exact(gate) * val).astype(BF16)

    x, lhs = prologue()
    for c in range(D_FF // FFN_CHUNK):
        up_chunk(lhs, c)
    y = x + jnp.dot(act_ref[...], wdn_ref[...], preferred_element_type=F32)
    if final_norm:
        y = _rms_scale(y) * gf_ref[...]
    o_ref[...] = regroup(y)


def _ffn(x2d, seq, g2, wup, cw, cb, wdn, gf, final_norm):
    m = x2d.shape[0]
    tm = TM_FFN
    assert seq % tm == 0
    tiles_per_seq = seq // tm
    hb = tm // FFN_HALO
    n_hb = m // FFN_HALO
    const = lambda i: (0, 0)
    full = lambda arr: pl.BlockSpec(arr.shape, const)
    resident = lambda arr: pl.BlockSpec(arr.shape, const, pipeline_mode=pl.Buffered(1))
    return pl.pallas_call(
        functools.partial(_ffn_kernel, tiles_per_seq=tiles_per_seq, final_norm=final_norm),
        grid=(m // tm,),
        in_specs=[
            pl.BlockSpec((tm, D_MODEL), lambda i: (i, 0)),
            pl.BlockSpec((FFN_HALO, D_MODEL), lambda i: (jnp.maximum(i * hb - 1, 0), 0)),
            pl.BlockSpec((FFN_HALO, D_MODEL), lambda i: (jnp.minimum((i + 1) * hb, n_hb - 1), 0)),
            full(g2), resident(wup), full(cw), full(cb), resident(wdn), full(gf),
        ],
        out_specs=pl.BlockSpec((tm, D_MODEL), lambda i: (i, 0)),
        out_shape=jax.ShapeDtypeStruct((m, D_MODEL), F32),
        scratch_shapes=[
            pltpu.VMEM((tm, D_FF), BF16),
            pltpu.VMEM((D_MODEL // LANES, tm, LANES), F32),
        ],
        compiler_params=pltpu.CompilerParams(
            dimension_semantics=("arbitrary",), vmem_limit_bytes=VMEM_LIMIT_BYTES),
        name="ffn",
    )(x2d, x2d, x2d, g2, wup, cw, cb, wdn, gf)


def kernel(x, norm1_g, w_in, b_in, rpb, w_na_out, conv_dw_w, conv_dw_b, conv_ln_g, conv_ln_b,
           w_conv_out, w_out, norm2_g, w_up, ffn_dw_w, ffn_dw_b, w_down, norm_f_g):
    bsz, seq, d = x.shape
    depth = w_in.shape[0]
    n_tok = 3 * D_ATTN + 2 * CONV_CH
    row = lambda v: v.reshape(1, -1).astype(F32)
    x2d = x.reshape(bsz * seq, d)
    for l in range(depth):
        w_in_l = w_in[l].astype(BF16)
        b_in_l = row(b_in[l])
        g1 = row(norm1_g[l])
        q, k, v, u = _inproj(x2d, g1, w_in_l[:, :n_tok], b_in_l[:, :n_tok])
        shp = (bsz, seq, D_ATTN)
        a = _natten(q.reshape(shp), k.reshape(shp), v.reshape(shp), _natten_bias_table(rpb[l]))
        x2d = _mixer(
            x2d, a.reshape(bsz * seq, D_ATTN), u, seq, g1, w_in_l[:, n_tok:], b_in_l[:, n_tok:],
            conv_dw_w[l].astype(F32), row(conv_dw_b[l]), row(conv_ln_g[l]), row(conv_ln_b[l]),
            w_na_out[l].astype(BF16), w_conv_out[l].astype(BF16), w_out[l].astype(BF16))
        x2d = _ffn(
            x2d, seq, row(norm2_g[l]), w_up[l].astype(BF16), ffn_dw_w[l].astype(F32),
            row(ffn_dw_b[l]), (0.5 * w_down[l]).astype(BF16), row(norm_f_g),
            final_norm=(l == depth - 1))
    return x2d.reshape(bsz, seq, d)
```

```python
import functools

import jax
import jax.numpy as jnp
import numpy as np
from jax import lax
from jax.experimental import pallas as pl
from jax.experimental.pallas import tpu as pltpu

F32 = jnp.float32
BF16 = jnp.bfloat16

D_MODEL = 1024
GRID_W = 64
NA_HEADS = 8
NA_HEAD_DIM = 64
D_ATTN = NA_HEADS * NA_HEAD_DIM
NA_KH = 8
NA_KW = 16
CONV_CH = 512
CONV_K = 31
D_FF = 2816
FFN_CONV_K = 3
EPS = 1e-6
NEG = -1e30
LOG2_E = 1.4426950408889634

LANES = 128
SUBLANES = 8
BF16_SUBLANES = 16
VMEM_LIMIT_BYTES = 56 * 1024 * 1024

TM_INPROJ = 1024
TM_MIXER = 512
TM_FFN = 512
NATTEN_ROWS_PER_STEP = 16
NATTEN_ROWS_PER_ITER = 16
NATTEN_PIPE_LAG = 3
CONV_VREG_BLOCK = 8
GATE_CHUNK = 1024
CONV_HALO = 16
FFN_HALO = 8
FFN_CHUNK = 256
FFN_ROW_PARTS = 2


def _sigmoid(x):
    return 0.5 * jnp.tanh(0.5 * x) + 0.5


def _rms_inv(x):
    return lax.rsqrt(jnp.mean(x * x, axis=-1, keepdims=True) + EPS)


def _rms_scale(x):
    return x * _rms_inv(x)


def _twice_gelu_exact(x):
    return x * (1.0 + lax.erf(x * np.float32(np.sqrt(0.5))))


def _inproj_kernel(x_ref, g_ref, w_ref, b_ref, q_ref, k_ref, v_ref, u_ref):
    h = (_rms_scale(x_ref[...]) * g_ref[...]).astype(BF16)

    def proj(n):
        cols = slice(n * D_ATTN, (n + 1) * D_ATTN)
        return jnp.dot(h, w_ref[:, cols], preferred_element_type=F32) + b_ref[:, cols]

    q_ref[...] = (proj(0) * (NA_HEAD_DIM ** -0.5 * LOG2_E)).astype(BF16)
    k_ref[...] = proj(1).astype(BF16)
    v_ref[...] = proj(2).astype(BF16)
    u_ref[...] = (proj(3) * _sigmoid(proj(4))).astype(BF16)


def _inproj(x2d, g, w, b):
    m = x2d.shape[0]
    tm = TM_INPROJ
    n_in = w.shape[1]
    const = lambda i: (0, 0)
    out = jax.ShapeDtypeStruct((m, D_ATTN), BF16)
    row_spec = pl.BlockSpec((tm, D_ATTN), lambda i: (i, 0))
    return pl.pallas_call(
        _inproj_kernel,
        grid=(m // tm,),
        in_specs=[
            pl.BlockSpec((tm, D_MODEL), lambda i: (i, 0)),
            pl.BlockSpec((1, D_MODEL), const),
            pl.BlockSpec((D_MODEL, n_in), const),
            pl.BlockSpec((1, n_in), const),
        ],
        out_specs=[row_spec, row_spec, row_spec, row_spec],
        out_shape=[out, out, out, out],
        compiler_params=pltpu.CompilerParams(
            dimension_semantics=("arbitrary",), vmem_limit_bytes=VMEM_LIMIT_BYTES),
        name="inproj",
    )(x2d, g, w, b)


def _natten_bias_table(rpb):
    col = jnp.arange(GRID_W)
    col_start = jnp.clip(col - NA_KW // 2, 0, GRID_W - NA_KW)
    kc = col[None, :]
    in_win = (kc >= col_start[:, None]) & (kc < col_start[:, None] + NA_KW)
    reach = GRID_W - NA_KW
    padded = jnp.pad(rpb.astype(F32) * LOG2_E, ((0, 0), (0, 0), (reach, reach)))
    t = jnp.stack([padded[:, :, GRID_W - 1 - c:2 * GRID_W - 1 - c] for c in range(GRID_W)], axis=2)
    t = jnp.where(in_win[None, None], t, NEG)
    nxt = jnp.concatenate([t[:, 1:], jnp.full_like(t[:, :1], NEG)], axis=1)
    return jnp.concatenate([t, nxt], axis=-1)


def _natten_kernel(q_ref, k_ref, v_ref, tb_ref, o_ref, *, rows):
    step = pl.program_id(1)
    lane = lax.broadcasted_iota(jnp.int32, (GRID_W, LANES), 1)
    lo_half = lane < NA_HEAD_DIM
    n_keys = NA_KH * GRID_W

    def row_group(it, carry):
        jobs = []
        for sub in range(NATTEN_ROWS_PER_ITER):
            rr = it * NATTEN_ROWS_PER_ITER + sub
            r = step * NATTEN_ROWS_PER_STEP + rr
            rs = jnp.clip(r - NA_KH // 2, 0, rows - NA_KH)
            base = rs - r + NA_KH - 1
            q0 = pl.multiple_of(rr * GRID_W, GRID_W)
            k0 = pl.multiple_of(rs * GRID_W, GRID_W)
            jobs += [(q0, k0, base, p) for p in range(NA_HEADS // 2)]

        def score(job):
            q0, k0, base, p = job
            cols = slice(p * LANES, (p + 1) * LANES)
            qp = q_ref[0, pl.ds(q0, GRID_W), cols]
            kp = k_ref[0, pl.ds(k0, n_keys), cols]
            zero = jnp.zeros_like(qp)
            q2 = jnp.concatenate([jnp.where(lo_half, qp, zero), jnp.where(lo_half, zero, qp)], axis=0)
            s = lax.dot_general(q2, kp, (((1,), (1,)), ((), ())), preferred_element_type=F32)
            bias = jnp.concatenate(
                [jnp.concatenate([tb_ref[2 * p + hh, base + 2 * j] for j in range(NA_KH // 2)], axis=1)
                 for hh in range(2)], axis=0)
            return s + bias

        def softmax(s):
            e = jnp.exp2(s - jnp.max(s, axis=-1, keepdims=True))
            return e.astype(BF16), jnp.sum(e, axis=-1, keepdims=True)

        def weighted_values(prob, job):
            e, l = prob
            q0, k0, base, p = job
            vp = v_ref[0, pl.ds(k0, n_keys), p * LANES:(p + 1) * LANES]
            o2 = jnp.dot(e, vp, preferred_element_type=F32) / l
            return jnp.where(lo_half, o2[:GRID_W], o2[GRID_W:])

        n_jobs = len(jobs)
        scores, outs = {}, []
        for tick in range(n_jobs + NATTEN_PIPE_LAG):
            if tick < n_jobs:
                scores[tick] = score(jobs[tick])
            j = tick - NATTEN_PIPE_LAG
            if 0 <= j < n_jobs:
                outs.append(weighted_values(softmax(scores.pop(j)), jobs[j]))

        pairs = NA_HEADS // 2
        for sub in range(NATTEN_ROWS_PER_ITER):
            q0 = jobs[sub * pairs][0]
            o_ref[0, pl.ds(q0, GRID_W), :] = jnp.concatenate(
                outs[sub * pairs:(sub + 1) * pairs], axis=1).astype(o_ref.dtype)
        return carry

    lax.fori_loop(0, NATTEN_ROWS_PER_STEP // NATTEN_ROWS_PER_ITER, row_group, 0)


def _natten(q, k, v, table):
    bsz, s, _ = q.shape
    rows = s // GRID_W
    assert rows >= NA_KH and rows % NATTEN_ROWS_PER_STEP == 0
    tq = NATTEN_ROWS_PER_STEP * GRID_W
    return pl.pallas_call(
        functools.partial(_natten_kernel, rows=rows),
        grid=(bsz, rows // NATTEN_ROWS_PER_STEP),
        in_specs=[
            pl.BlockSpec((1, tq, D_ATTN), lambda b, i: (b, i, 0)),
            pl.BlockSpec((1, s, D_ATTN), lambda b, i: (b, 0, 0)),
            pl.BlockSpec((1, s, D_ATTN), lambda b, i: (b, 0, 0)),
            pl.BlockSpec(table.shape, lambda b, i: (0, 0, 0, 0)),
        ],
        out_specs=pl.BlockSpec((1, tq, D_ATTN), lambda b, i: (b, i, 0)),
        out_shape=jax.ShapeDtypeStruct((bsz, s, D_ATTN), BF16),
        compiler_params=pltpu.CompilerParams(
            dimension_semantics=("arbitrary", "arbitrary"), vmem_limit_bytes=VMEM_LIMIT_BYTES),
        name="natten",
    )(q, k, v, table)


def _mixer_kernel(x_ref, a_ref, u_ref, up_ref, un_ref, g1_ref, wg_ref, bg_ref, cw_ref, cb_ref,
                  lg_ref, lb_ref, wna_ref, wco_ref, wo_ref, pin_ref, o_ref, d_ref, y_ref, u2_ref,
                  gates_ref, h_ref, *, tiles_per_seq):
    tm = x_ref.shape[0]
    run = tm // SUBLANES
    pad = CONV_K // 2
    n_lb = CONV_CH // LANES
    t = pl.program_id(0) % tiles_per_seq
    prev = up_ref[...].astype(F32) * (t != 0).astype(F32)
    nxt = un_ref[...].astype(F32) * (t != tiles_per_seq - 1).astype(F32)

    d_main = jnp.dot(pin_ref[...], u_ref[...], preferred_element_type=F32)
    sub = lax.broadcasted_iota(jnp.int32, (SUBLANES, LANES), 0)
    vreg = lambda v, cols: d_main[v * SUBLANES:(v + 1) * SUBLANES, cols]
    for lb in range(n_lb):
        cols = slice(lb * LANES, (lb + 1) * LANES)
        d_ref[lb, pad * SUBLANES:pad * SUBLANES + tm, :] = d_main[:, cols]
        for j in range(pad):
            lo = jnp.where(sub == 0, prev[CONV_HALO - pad + j:CONV_HALO - pad + j + 1, cols],
                           pltpu.roll(vreg(run - pad + j, cols), 1, axis=0))
            d_ref[lb, j * SUBLANES:(j + 1) * SUBLANES, :] = lo
            hi = jnp.where(sub == SUBLANES - 1, nxt[j:j + 1, cols],
                           pltpu.roll(vreg(j, cols), SUBLANES - 1, axis=0))
            d_ref[lb, (pad + run + j) * SUBLANES:(pad + run + j + 1) * SUBLANES, :] = hi

    x = x_ref[...]
    h_ref[...] = (_rms_scale(x) * g1_ref[...]).astype(BF16)

    n_acc = CONV_VREG_BLOCK
    blocks_per_lb = run // n_acc
    n_chunks = gates_ref.shape[0]
    blocks_per_chunk = n_lb * blocks_per_lb // n_chunks

    def conv_block(blk):
        lb = blk // blocks_per_lb
        row0 = pl.multiple_of((blk % blocks_per_lb) * (n_acc * SUBLANES), n_acc * SUBLANES)
        w = [cw_ref[lb, kk:kk + 1, :] for kk in range(CONV_K)]
        accs = [cb_ref[lb] + jnp.zeros((SUBLANES, LANES), F32)] * n_acc
        for i in range(n_acc + CONV_K - 1):
            dv = d_ref[lb, pl.ds(row0 + i * SUBLANES, SUBLANES), :]
            for j in range(n_acc):
                if 0 <= i - j < CONV_K:
                    accs[j] = accs[j] + w[i - j] * dv
        return lb, row0, accs

    def chunk(c, carry):
        convs = [conv_block(c * blocks_per_chunk + b) for b in range(blocks_per_chunk)]
        g = jnp.dot(h_ref[...], wg_ref[c], preferred_element_type=F32) + bg_ref[c]
        for lb, row0, accs in convs:
            for j in range(n_acc):
                y_ref[lb, pl.ds(row0 + j * SUBLANES, SUBLANES), :] = accs[j]
        gates_ref[c] = g
        return carry

    lax.fori_loop(0, n_chunks, chunk, 0)

    half = n_chunks // 2
    gate = lambda first: _sigmoid(
        jnp.concatenate([gates_ref[first + n] for n in range(half)], axis=1))
    gated_a = gate(0) * jnp.dot(a_ref[...], wna_ref[...], preferred_element_type=F32)

    y = jnp.concatenate([y_ref[lb] for lb in range(n_lb)], axis=1)
    mu = jnp.mean(y, axis=-1, keepdims=True)
    dlt = y - mu
    var = jnp.mean(dlt * dlt, axis=-1, keepdims=True)
    z = dlt * lax.rsqrt(var + EPS) * lg_ref[...] + lb_ref[...]
    z = z * _sigmoid(z)
    for lb in range(n_lb):
        y_ref[lb] = z[:, lb * LANES:(lb + 1) * LANES]
    for lb in range(n_lb):
        for s in range(SUBLANES):
            for v0 in range(0, run, BF16_SUBLANES):
                u2_ref[s * run + v0:s * run + v0 + BF16_SUBLANES, lb * LANES:(lb + 1) * LANES] = (
                    y_ref[lb, pl.ds(v0 * SUBLANES + s, BF16_SUBLANES, stride=SUBLANES), :].astype(BF16))

    br_b = jnp.dot(u2_ref[...], wco_ref[...], preferred_element_type=F32)
    merged = gated_a + gate(half) * br_b
    o_ref[...] = x + jnp.dot(merged.astype(BF16), wo_ref[...], preferred_element_type=F32)


def _conv_layout_selector(tm):
    run = tm // SUBLANES
    v, s = np.meshgrid(np.arange(run), np.arange(SUBLANES), indexing="ij")
    p_in = np.zeros((tm, tm), np.float32)
    p_in[(v * SUBLANES + s).ravel(), (s * run + v).ravel()] = 1.0
    return jnp.asarray(p_in, BF16)


def _mixer(x2d, a2d, u2d, seq, g1, wg, bg, cw, cb, lg, lb, wna, wco, wo):
    m = x2d.shape[0]
    tm = TM_MIXER
    run = tm // SUBLANES
    assert seq % tm == 0 and run % BF16_SUBLANES == 0 and run % CONV_VREG_BLOCK == 0
    tiles_per_seq = seq // tm
    hb = tm // CONV_HALO
    n_hb = m // CONV_HALO
    n_lb = CONV_CH // LANES
    n_chunks = 2 * D_MODEL // GATE_CHUNK
    assert (n_lb * (run // CONV_VREG_BLOCK)) % n_chunks == 0
    wg = wg.reshape(D_MODEL, n_chunks, GATE_CHUNK).transpose(1, 0, 2)
    bg = bg.reshape(n_chunks, 1, GATE_CHUNK)
    cw = cw.reshape(CONV_K, n_lb, LANES).transpose(1, 0, 2)
    cb = cb.reshape(n_lb, 1, LANES)
    p_in = _conv_layout_selector(tm)
    full = lambda arr: pl.BlockSpec(arr.shape, lambda i: (0,) * arr.ndim)
    return pl.pallas_call(
        functools.partial(_mixer_kernel, tiles_per_seq=tiles_per_seq),
        grid=(m // tm,),
        in_specs=[
            pl.BlockSpec((tm, D_MODEL), lambda i: (i, 0)),
            pl.BlockSpec((tm, D_ATTN), lambda i: (i, 0)),
            pl.BlockSpec((tm, CONV_CH), lambda i: (i, 0)),
            pl.BlockSpec((CONV_HALO, CONV_CH), lambda i: (jnp.maximum(i * hb - 1, 0), 0)),
            pl.BlockSpec((CONV_HALO, CONV_CH), lambda i: (jnp.minimum((i + 1) * hb, n_hb - 1), 0)),
            full(g1), full(wg), full(bg), full(cw), full(cb), full(lg), full(lb),
            full(wna), full(wco), full(wo), full(p_in),
        ],
        out_specs=pl.BlockSpec((tm, D_MODEL), lambda i: (i, 0)),
        out_shape=jax.ShapeDtypeStruct((m, D_MODEL), F32),
        scratch_shapes=[
            pltpu.VMEM((n_lb, tm + 2 * (CONV_K // 2) * SUBLANES, LANES), F32),
            pltpu.VMEM((n_lb, tm, LANES), F32),
            pltpu.VMEM((tm, CONV_CH), BF16),
            pltpu.VMEM((n_chunks, tm, GATE_CHUNK), F32),
            pltpu.VMEM((tm, D_MODEL), BF16),
        ],
        compiler_params=pltpu.CompilerParams(
            dimension_semantics=("arbitrary",), vmem_limit_bytes=VMEM_LIMIT_BYTES),
        name="mixer",
    )(x2d, a2d, u2d, u2d, u2d, g1, wg, bg, cw, cb, lg, lb, wna, wco, wo, p_in)


def _ffn_kernel(x_ref, xp_ref, xn_ref, g2_ref, wup_ref, cw_ref, cb_ref, wdn_ref, gf_ref, o_ref,
                act_ref, xs_ref, *, tiles_per_seq, final_norm):
    assert FFN_CONV_K == 3 and FFN_HALO == SUBLANES
    ts = x_ref.shape[0]
    n_lb = D_MODEL // LANES
    group = SUBLANES * SUBLANES
    n_groups = ts // group
    t = pl.program_id(0) % tiles_per_seq
    lanes = lambda lb: slice(lb * LANES, (lb + 1) * LANES)
    sub = lax.broadcasted_iota(jnp.int32, (SUBLANES, FFN_CHUNK), 0)
    g2 = g2_ref[...]

    def regroup(rows):
        for lb in range(n_lb):
            xs_ref[lb] = rows[:, lanes(lb)]
        return jnp.concatenate([
            jnp.concatenate([xs_ref[lb, pl.ds(k * SUBLANES - (k % SUBLANES) * (SUBLANES - 1),
                                              SUBLANES, stride=SUBLANES), :]
                             for lb in range(n_lb)], axis=1)
            for k in range(ts // SUBLANES)], axis=0)

    norm = lambda rows: _rms_scale(rows) * g2

    def prologue():
        before = norm(xp_ref[...]) * (t != 0).astype(F32)
        after = norm(xn_ref[...]) * (t != tiles_per_seq - 1).astype(F32)
        x = regroup(x_ref[...])
        h_ext = jnp.concatenate([
            (x * g2).astype(BF16), jnp.concatenate([before, after], axis=0).astype(BF16)], axis=0)
        r_ext = jnp.concatenate([_rms_inv(x), jnp.ones((2 * FFN_HALO, 1), F32)], axis=0)
        return x, (h_ext, r_ext)

    def up_chunk(lhs, c):
        h_ext, r_ext = lhs

        def project(col0):
            cols = slice(col0, col0 + FFN_CHUNK)
            return r_ext * jnp.dot(h_ext, wup_ref[:, cols], preferred_element_type=F32)

        def conv(up, col0, g_lo, g_hi):
            cols = slice(col0, col0 + FFN_CHUNK)
            vreg = lambda k: up[k * SUBLANES:(k + 1) * SUBLANES]
            last = lambda g: vreg(ts // SUBLANES if g < 0 else g * SUBLANES + SUBLANES - 1)
            first = lambda g: vreg(ts // SUBLANES + 1 if g == n_groups else g * SUBLANES)
            prev, nxt = [], []
            for g in range(g_lo, g_hi):
                lo, hi = g * group, (g + 1) * group
                prev += [jnp.where(sub == 0, pltpu.roll(last(g - 1), 1, axis=0),
                                   pltpu.roll(last(g), 1, axis=0)), up[lo:hi - SUBLANES]]
                nxt += [up[lo + SUBLANES:hi],
                        jnp.where(sub == SUBLANES - 1, pltpu.roll(first(g + 1), SUBLANES - 1, axis=0),
                                  pltpu.roll(first(g), SUBLANES - 1, axis=0))]
            rows = slice(g_lo * group, g_hi * group)
            return (cb_ref[:, cols] + cw_ref[0:1, cols] * jnp.concatenate(prev, axis=0)
                    + cw_ref[1:2, cols] * up[rows] + cw_ref[2:3, cols] * jnp.concatenate(nxt, axis=0))

        up_gate = project(c * FFN_CHUNK)
        up_val = project(D_FF + c * FFN_CHUNK)
        per_part = n_groups // FFN_ROW_PARTS
        for part in range(FFN_ROW_PARTS):
            g_lo, g_hi = part * per_part, (part + 1) * per_part
            gate = conv(up_gate, c * FFN_CHUNK, g_lo, g_hi)
            val = conv(up_val, D_FF + c * FFN_CHUNK, g_lo, g_hi)
            act_ref[g_lo * group:g_hi * group, c * FFN_CHUNK:(c + 1) * FFN_CHUNK] = (
                _twice_gelu_exact(gate) * val).astype(BF16)

    x, lhs = prologue()
    for c in range(D_FF // FFN_CHUNK):
        up_chunk(lhs, c)
    y = x + jnp.dot(act_ref[...], wdn_ref[...], preferred_element_type=F32)
    if final_norm:
        y = _rms_scale(y) * gf_ref[...]
    o_ref[...] = regroup(y)


def _ffn(x2d, seq, g2, wup, cw, cb, wdn, gf, final_norm):
    m = x2d.shape[0]
    tm = TM_FFN
    assert seq % tm == 0
    tiles_per_seq = seq // tm
    hb = tm // FFN_HALO
    n_hb = m // FFN_HALO
    const = lambda i: (0, 0)
    full = lambda arr: pl.BlockSpec(arr.shape, const)
    resident = lambda arr: pl.BlockSpec(arr.shape, const, pipeline_mode=pl.Buffered(1))
    return pl.pallas_call(
        functools.partial(_ffn_kernel, tiles_per_seq=tiles_per_seq, final_norm=final_norm),
        grid=(m // tm,),
        in_specs=[
            pl.BlockSpec((tm, D_MODEL), lambda i: (i, 0)),
            pl.BlockSpec((FFN_HALO, D_MODEL), lambda i: (jnp.maximum(i * hb - 1, 0), 0)),
            pl.BlockSpec((FFN_HALO, D_MODEL), lambda i: (jnp.minimum((i + 1) * hb, n_hb - 1), 0)),
            full(g2), resident(wup), full(cw), full(cb), resident(wdn), full(gf),
        ],
        out_specs=pl.BlockSpec((tm, D_MODEL), lambda i: (i, 0)),
        out_shape=jax.ShapeDtypeStruct((m, D_MODEL), F32),
        scratch_shapes=[
            pltpu.VMEM((tm, D_FF), BF16),
            pltpu.VMEM((D_MODEL // LANES, tm, LANES), F32),
        ],
        compiler_params=pltpu.CompilerParams(
            dimension_semantics=("arbitrary",), vmem_limit_bytes=VMEM_LIMIT_BYTES),
        name="ffn",
    )(x2d, x2d, x2d, g2, wup, cw, cb, wdn, gf)


def kernel(x, norm1_g, w_in, b_in, rpb, w_na_out, conv_dw_w, conv_dw_b, conv_ln_g, conv_ln_b,
           w_conv_out, w_out, norm2_g, w_up, ffn_dw_w, ffn_dw_b, w_down, norm_f_g):
    bsz, seq, d = x.shape
    depth = w_in.shape[0]
    n_tok = 3 * D_ATTN + 2 * CONV_CH
    row = lambda v: v.reshape(1, -1).astype(F32)
    x2d = x.reshape(bsz * seq, d)
    for l in range(depth):
        w_in_l = w_in[l].astype(BF16)
        b_in_l = row(b_in[l])
        g1 = row(norm1_g[l])
        q, k, v, u = _inproj(x2d, g1, w_in_l[:, :n_tok], b_in_l[:, :n_tok])
        shp = (bsz, seq, D_ATTN)
        a = _natten(q.reshape(shp), k.reshape(shp), v.reshape(shp), _natten_bias_table(rpb[l]))
        x2d = _mixer(
            x2d, a.reshape(bsz * seq, D_ATTN), u, seq, g1, w_in_l[:, n_tok:], b_in_l[:, n_tok:],
            conv_dw_w[l].astype(F32), row(conv_dw_b[l]), row(conv_ln_g[l]), row(conv_ln_b[l]),
            w_na_out[l].astype(BF16), w_conv_out[l].astype(BF16), w_out[l].astype(BF16))
        x2d = _ffn(
            x2d, seq, row(norm2_g[l]), w_up[l].astype(BF16), ffn_dw_w[l].astype(F32),
            row(ffn_dw_b[l]), (0.5 * w_down[l]).astype(BF16), row(norm_f_g),
            final_norm=(l == depth - 1))
    return x2d.reshape(bsz, seq, d)
```
